```python
import math
import jax, jax.numpy as jnp
from jax import lax
import numpy as np

D_MODEL = 1024
BATCH = 2
SEQ = 16384
DEPTH = 4

N_MIXERS = 4
ROPE_THETA = 500000.0
NORM_EPS = 1e-6
QBLK = 128
NEG_INF = -1e30
POS_BIG = 1e30
D_FF = ((-(-8 * D_MODEL // 3) + 255) // 256) * 256

NSA_HEAD_DIM = 64
NSA_HEADS = D_MODEL // NSA_HEAD_DIM
NSA_KV_HEADS = 4
CMP_LEN = 32
CMP_STRIDE = 16
CMP_HIDDEN = 4 * NSA_HEAD_DIM
SEL_LEN = 64
SEL_TOP = 16
NSA_WIN = 512

MLA_HEADS = 16
MLA_NOPE = 64
MLA_ROPE = 32
MLA_V = 64
MLA_Q_LORA = 768
MLA_KV_LORA = 256

SWA_HEADS = 16
SWA_KV_HEADS = 2
SWA_HEAD_DIM = 64
SWA_WIN = 128

DIFF_HEAD_DIM = 64
DIFF_HEADS = D_MODEL // (2 * DIFF_HEAD_DIM)

kernel_name = "hybrid_nsa_mla_swa_diff_interleaved"


def rms_norm(x, g):
    xf = x.astype(jnp.float32)
    y = xf * lax.rsqrt(jnp.mean(xf * xf, axis=-1, keepdims=True) + NORM_EPS)
    return (y * g.astype(jnp.float32)).astype(x.dtype)


def rope(x, pos, rot_dim):
    half = rot_dim // 2
    inv_freq = ROPE_THETA ** (-jnp.arange(half, dtype=jnp.float32) / half)
    ang = pos.astype(jnp.float32)[:, :, None] * inv_freq
    cos = jnp.cos(ang)[:, :, None, :]
    sin = jnp.sin(ang)[:, :, None, :]
    xf = x.astype(jnp.float32)
    x1, x2, rest = xf[..., :half], xf[..., half:rot_dim], xf[..., rot_dim:]
    return jnp.concatenate([x1 * cos - x2 * sin, x2 * cos + x1 * sin, rest], axis=-1).astype(x.dtype)


def masked_softmax(s, valid):
    return jax.nn.softmax(jnp.where(valid, s.astype(jnp.float32), NEG_INF), axis=-1)


def sweep_query_blocks(block_fn, seq):
    out = lax.map(block_fn, jnp.arange(seq // QBLK))
    b = out.shape[1]
    return jnp.swapaxes(out, 0, 1).reshape(b, seq, out.shape[-1])


def swiglu(h, w_gate, w_up, w_down):
    return (jax.nn.silu(h @ w_gate) * (h @ w_up)) @ w_down


def compress_blocks(t, pe, w1, w2):
    B, S, G, hd = t.shape
    ch = t.reshape(B, S // CMP_STRIDE, CMP_STRIDE, G, hd)
    blk = jnp.concatenate([ch[:, :-1], ch[:, 1:]], axis=2) + pe[:, None, :]
    blk = jnp.moveaxis(blk, 3, 2).reshape(B, -1, G, CMP_LEN * hd)
    return jax.nn.gelu(blk @ w1) @ w2


def nsa_mixer(h, pos, w_in, cmp_pe_k, cmp_w1_k, cmp_w2_k, cmp_pe_v, cmp_w1_v, cmp_w2_v, w_out):
    B, S, _ = h.shape
    H, G, hd = NSA_HEADS, NSA_KV_HEADS, NSA_HEAD_DIM
    hpg = H // G
    n_cmp = S // CMP_STRIDE - 1
    n_sel = S // SEL_LEN
    n_top = min(SEL_TOP, n_sel)
    scale = hd ** -0.5
    rot = hd // 4
    kvw = G * hd
    cuts = [H * hd + j * kvw for j in range(7)]
    q, kc, vc, ks, vs, kw, vw, g = jnp.split(h @ w_in, cuts, axis=-1)
    q = rope(q.reshape(B, S, H, hd), pos, rot)
    kc, vc, ks, vs, kw, vw = (t.reshape(B, S, G, hd) for t in (kc, vc, ks, vs, kw, vw))
    ks = rope(ks, pos, rot)
    kw = rope(kw, pos, rot)
    gates = jax.nn.sigmoid(g.astype(jnp.float32)).reshape(B, S, G, hpg, 3)
    pos_cmp = pos.reshape(B, S // CMP_STRIDE, CMP_STRIDE)[:, 1:, -1]
    k_cmp = rope(compress_blocks(kc, cmp_pe_k, cmp_w1_k, cmp_w2_k), pos_cmp, rot)
    v_cmp = compress_blocks(vc, cmp_pe_v, cmp_w1_v, cmp_w2_v)
    k_selb = jnp.moveaxis(ks.reshape(B, n_sel, SEL_LEN, G, hd), 3, 1)
    v_selb = jnp.moveaxis(vs.reshape(B, n_sel, SEL_LEN, G, hd), 3, 1)
    padw = ((0, 0), (NSA_WIN, 0), (0, 0), (0, 0))
    k_winp = jnp.pad(kw, padw)
    v_winp = jnp.pad(vw, padw)
    cmp_idx = jnp.arange(n_cmp)
    sel_idx = jnp.arange(n_sel)
    cmp_end = cmp_idx * CMP_STRIDE + CMP_LEN - 1
    sel_start = sel_idx * SEL_LEN
    cover = ((cmp_idx[:, None] * CMP_STRIDE <= sel_start[None, :] + SEL_LEN - 1)
             & (cmp_end[:, None] >= sel_start[None, :])).astype(jnp.float32)
    b_ix = jnp.arange(B)[:, None, None, None]
    g_ix = jnp.arange(G)[None, :, None, None]

    def block(c):
        q0 = c * QBLK
        qi = q0 + jnp.arange(QBLK)
        qb = lax.dynamic_slice_in_dim(q, q0, QBLK, 1).reshape(B, QBLK, G, hpg, hd)
        s = jnp.einsum('bqghd,bngd->bghqn', qb, k_cmp) * scale
        valid = cmp_end[None, :] <= qi[:, None]
        p = masked_softmax(s, valid) * valid
        o_cmp = jnp.einsum('bghqn,bngd->bqghd', p.astype(v_cmp.dtype), v_cmp)
        imp = jnp.einsum('bghqn,nj->bgqj', p, cover)
        forced = (sel_idx[None, :] == (qi // SEL_LEN)[:, None]) | (sel_idx[None, :] == 0)
        imp = jnp.where(forced, POS_BIG, imp)
        imp = jnp.where(sel_start[None, :] <= qi[:, None], imp, NEG_INF)
        _, top = lax.top_k(imp, n_top)
        k_g = k_selb[b_ix, g_ix, top].reshape(B, G, QBLK, n_top * SEL_LEN, hd)
        v_g = v_selb[b_ix, g_ix, top].reshape(B, G, QBLK, n_top * SEL_LEN, hd)
        tok = (top[..., None] * SEL_LEN + jnp.arange(SEL_LEN)).reshape(B, G, QBLK, n_top * SEL_LEN)
        s = jnp.einsum('bqghd,bgqkd->bghqk', qb, k_g) * scale
        valid = (tok <= qi[None, None, :, None])[:, :, None]
        p = masked_softmax(s, valid)
        o_sel = jnp.einsum('bghqk,bgqkd->bqghd', p.astype(v_g.dtype), v_g)
        k_w = lax.dynamic_slice_in_dim(k_winp, q0, QBLK + NSA_WIN, 1)
        v_w = lax.dynamic_slice_in_dim(v_winp, q0, QBLK + NSA_WIN, 1)
        ki = q0 - NSA_WIN + jnp.arange(QBLK + NSA_WIN)
        valid = (ki[None, :] >= 0) & (ki[None, :] <= qi[:, None]) & (ki[None, :] > qi[:, None] - NSA_WIN)
        s = jnp.einsum('bqghd,bkgd->bghqk', qb, k_w) * scale
        p = masked_softmax(s, valid)
        o_win = jnp.einsum('bghqk,bkgd->bqghd', p.astype(v_w.dtype), v_w)
        gb = lax.dynamic_slice_in_dim(gates, q0, QBLK, 1)
        o = gb[..., 0:1] * o_cmp + gb[..., 1:2] * o_sel + gb[..., 2:3] * o_win
        return o.reshape(B, QBLK, H * hd).astype(h.dtype)

    return sweep_query_blocks(block, S) @ w_out


def mla_mixer(h, pos, w_in, q_norm_g, kv_norm_g, w_q_up, w_kv_up, w_out):
    B, S, _ = h.shape
    H = MLA_HEADS
    dk = MLA_NOPE + MLA_ROPE
    q_lat, kv_lat, k_pe = jnp.split(h @ w_in, [MLA_Q_LORA, MLA_Q_LORA + MLA_KV_LORA], axis=-1)
    q = (rms_norm(q_lat, q_norm_g) @ w_q_up).reshape(B, S, H, dk)
    q = jnp.concatenate([q[..., :MLA_NOPE], rope(q[..., MLA_NOPE:], pos, MLA_ROPE)], axis=-1)
    kv = (rms_norm(kv_lat, kv_norm_g) @ w_kv_up).reshape(B, S, H, MLA_NOPE + MLA_V)
    k_pe = rope(k_pe[:, :, None, :], pos, MLA_ROPE)
    k = jnp.concatenate([kv[..., :MLA_NOPE], jnp.broadcast_to(k_pe, (B, S, H, MLA_ROPE))], axis=-1)
    v = kv[..., MLA_NOPE:]
    scale = dk ** -0.5
    kidx = jnp.arange(S)

    def block(c):
        q0 = c * QBLK
        qi = q0 + jnp.arange(QBLK)
        qb = lax.dynamic_slice_in_dim(q, q0, QBLK, 1)
        s = jnp.einsum('bqhd,bkhd->bhqk', qb, k) * scale
        p = masked_softmax(s, kidx[None, :] <= qi[:, None])
        o = jnp.einsum('bhqk,bkhd->bqhd', p.astype(v.dtype), v)
        return o.reshape(B, QBLK, H * MLA_V)

    return sweep_query_blocks(block, S) @ w_out


def swa_sink_mixer(h, pos, w_in, sinks, w_out):
    B, S, _ = h.shape
    H, G, hd, W = SWA_HEADS, SWA_KV_HEADS, SWA_HEAD_DIM, SWA_WIN
    hpg = H // G
    nb = S // W
    rot = hd // 4
    q, k, v = jnp.split(h @ w_in, [H * hd, H * hd + G * hd], axis=-1)
    q = rope(q.reshape(B, S, H, hd), pos, rot).reshape(B, nb, W, G, hpg, hd)
    k = rope(k.reshape(B, S, G, hd), pos, rot).reshape(B, nb, W, G, hd)
    v = v.reshape(B, nb, W, G, hd)

    def with_prev(t):
        prev = jnp.pad(t, ((0, 0), (1, 0), (0, 0), (0, 0), (0, 0)))[:, :-1]
        return jnp.concatenate([prev, t], axis=2)

    kk, vv = with_prev(k), with_prev(v)
    s = jnp.einsum('bnqghd,bnkgd->bnghqk', q, kk) * hd ** -0.5
    qi = jnp.arange(W)[:, None] + W
    ki = jnp.arange(2 * W)[None, :]
    band = (ki <= qi) & (ki > qi - W)
    real = (jnp.arange(nb)[:, None, None] > 0) | (ki >= W)[None]
    valid = (band[None] & real)[None, :, None, None]
    s = jnp.where(valid, s.astype(jnp.float32), NEG_INF)
    sink = jnp.broadcast_to(sinks.astype(jnp.float32).reshape(1, 1, G, hpg, 1, 1), s.shape[:-1] + (1,))
    p = jax.nn.softmax(jnp.concatenate([s, sink], axis=-1), axis=-1)[..., :-1]
    o = jnp.einsum('bnghqk,bnkgd->bnqghd', p.astype(vv.dtype), vv)
    return o.reshape(B, S, H * hd) @ w_out


def diff_mixer(h, pos, w_in, lam_q1, lam_k1, lam_q2, lam_k2, subln_g, w_out, lam_init):
    B, S, _ = h.shape
    H, d = DIFF_HEADS, DIFF_HEAD_DIM
    rot = d // 4
    q, k, v = jnp.split(h @ w_in, [H * 2 * d, 2 * H * 2 * d], axis=-1)
    q = q.reshape(B, S, H, 2, d)
    k = k.reshape(B, S, H, 2, d)
    q1, q2 = rope(q[..., 0, :], pos, rot), rope(q[..., 1, :], pos, rot)
    k1, k2 = rope(k[..., 0, :], pos, rot), rope(k[..., 1, :], pos, rot)
    v = v.reshape(B, S, H, 2 * d)
    lam = (jnp.exp(jnp.sum(lam_q1.astype(jnp.float32) * lam_k1.astype(jnp.float32)))
           - jnp.exp(jnp.sum(lam_q2.astype(jnp.float32) * lam_k2.astype(jnp.float32))) + lam_init)
    scale = d ** -0.5
    kidx = jnp.arange(S)

    def block(c):
        q0 = c * QBLK
        qi = q0 + jnp.arange(QBLK)
        causal = kidx[None, :] <= qi[:, None]
        p1 = masked_softmax(jnp.einsum('bqhd,bkhd->bhqk', lax.dynamic_slice_in_dim(q1, q0, QBLK, 1), k1) * scale, causal)
        p2 = masked_softmax(jnp.einsum('bqhd,bkhd->bhqk', lax.dynamic_slice_in_dim(q2, q0, QBLK, 1), k2) * scale, causal)
        o = jnp.einsum('bhqk,bkhd->bqhd', (p1 - lam * p2).astype(v.dtype), v)
        return o.reshape(B, QBLK, H * 2 * d)

    o = sweep_query_blocks(block, S).reshape(B, S, H, 2 * d)
    o = rms_norm(o, subln_g) * (1.0 - lam_init)
    return o.reshape(B, S, H * 2 * d) @ w_out


def setup_inputs(seed: int = 0) -> dict:
    key = jax.random.key(seed)
    keys = iter(jax.random.split(key, 48))

    def nrm(shape, scale):
        return scale * jax.random.normal(next(keys), shape, jnp.float32)

    def lin(shape):
        return nrm(shape, shape[-2] ** -0.5)

    def gain(shape):
        return 1.0 + nrm(shape, 0.1)

    L, D, F = DEPTH, D_MODEL, D_FF
    n_nsa, n_mla, n_swa, n_diff = (len(range(m, DEPTH, N_MIXERS)) for m in range(N_MIXERS))
    nsa_in = NSA_HEADS * NSA_HEAD_DIM + 6 * NSA_KV_HEADS * NSA_HEAD_DIM + 3 * NSA_HEADS
    x = jax.random.normal(next(keys), (BATCH, SEQ, D), jnp.float32)
    positions = jnp.broadcast_to(jnp.arange(SEQ, dtype=jnp.int32)[None, :], (BATCH, SEQ))
    return {
        "x": x,
        "positions": positions,
        "ln_mix_pre": gain((L, D)),
        "ln_mix_post": gain((L, D)),
        "ln_ffn_pre": gain((L, D)),
        "ln_ffn_post": gain((L, D)),
        "ffn_w_gate": lin((L, D, F)),
        "ffn_w_up": lin((L, D, F)),
        "ffn_w_down": lin((L, F, D)),
        "nsa_w_in": lin((n_nsa, D, nsa_in)),
        "nsa_cmp_pe_k": nrm((n_nsa, CMP_LEN, NSA_HEAD_DIM), 0.1),
        "nsa_cmp_w1_k": lin((n_nsa, CMP_LEN * NSA_HEAD_DIM, CMP_HIDDEN)),
        "nsa_cmp_w2_k": lin((n_nsa, CMP_HIDDEN, NSA_HEAD_DIM)),
        "nsa_cmp_pe_v": nrm((n_nsa, CMP_LEN, NSA_HEAD_DIM), 0.1),
        "nsa_cmp_w1_v": lin((n_nsa, CMP_LEN * NSA_HEAD_DIM, CMP_HIDDEN)),
        "nsa_cmp_w2_v": lin((n_nsa, CMP_HIDDEN, NSA_HEAD_DIM)),
        "nsa_w_out": lin((n_nsa, NSA_HEADS * NSA_HEAD_DIM, D)),
        "mla_w_in": lin((n_mla, D, MLA_Q_LORA + MLA_KV_LORA + MLA_ROPE)),
        "mla_q_norm": gain((n_mla, MLA_Q_LORA)),
        "mla_kv_norm": gain((n_mla, MLA_KV_LORA)),
        "mla_w_q_up": lin((n_mla, MLA_Q_LORA, MLA_HEADS * (MLA_NOPE + MLA_ROPE))),
        "mla_w_kv_up": lin((n_mla, MLA_KV_LORA, MLA_HEADS * (MLA_NOPE + MLA_V))),
        "mla_w_out": lin((n_mla, MLA_HEADS * MLA_V, D)),
        "swa_w_in": lin((n_swa, D, (SWA_HEADS + 2 * SWA_KV_HEADS) * SWA_HEAD_DIM)),
        "swa_sinks": nrm((n_swa, SWA_HEADS), 0.5),
        "swa_w_out": lin((n_swa, SWA_HEADS * SWA_HEAD_DIM, D)),
        "diff_w_in": lin((n_diff, D, 3 * DIFF_HEADS * 2 * DIFF_HEAD_DIM)),
        "diff_lam_q1": nrm((n_diff, DIFF_HEAD_DIM), 0.1),
        "diff_lam_k1": nrm((n_diff, DIFF_HEAD_DIM), 0.1),
        "diff_lam_q2": nrm((n_diff, DIFF_HEAD_DIM), 0.1),
        "diff_lam_k2": nrm((n_diff, DIFF_HEAD_DIM), 0.1),
        "diff_subln": gain((n_diff, 2 * DIFF_HEAD_DIM)),
        "diff_w_out": lin((n_diff, DIFF_HEADS * 2 * DIFF_HEAD_DIM, D)),
    }


def reference(x, positions, ln_mix_pre, ln_mix_post, ln_ffn_pre, ln_ffn_post, ffn_w_gate, ffn_w_up, ffn_w_down,
              nsa_w_in, nsa_cmp_pe_k, nsa_cmp_w1_k, nsa_cmp_w2_k, nsa_cmp_pe_v, nsa_cmp_w1_v, nsa_cmp_w2_v, nsa_w_out,
              mla_w_in, mla_q_norm, mla_kv_norm, mla_w_q_up, mla_w_kv_up, mla_w_out,
              swa_w_in, swa_sinks, swa_w_out,
              diff_w_in, diff_lam_q1, diff_lam_k1, diff_lam_q2, diff_lam_k2, diff_subln, diff_w_out):
    h = x
    for i in range(DEPTH):
        kind, j = i % N_MIXERS, i // N_MIXERS
        u = rms_norm(h, ln_mix_pre[i])
        if kind == 0:
            m = nsa_mixer(u, positions, nsa_w_in[j], nsa_cmp_pe_k[j], nsa_cmp_w1_k[j], nsa_cmp_w2_k[j],
                          nsa_cmp_pe_v[j], nsa_cmp_w1_v[j], nsa_cmp_w2_v[j], nsa_w_out[j])
        elif kind == 1:
            m = mla_mixer(u, positions, mla_w_in[j], mla_q_norm[j], mla_kv_norm[j], mla_w_q_up[j],
                          mla_w_kv_up[j], mla_w_out[j])
        elif kind == 2:
            m = swa_sink_mixer(u, positions, swa_w_in[j], swa_sinks[j], swa_w_out[j])
        else:
            lam_init = 0.8 - 0.6 * math.exp(-0.3 * i)
            m = diff_mixer(u, positions, diff_w_in[j], diff_lam_q1[j], diff_lam_k1[j], diff_lam_q2[j],
                           diff_lam_k2[j], diff_subln[j], diff_w_out[j], lam_init)
        h = h + rms_norm(m, ln_mix_post[i])
        u = rms_norm(h, ln_ffn_pre[i])
        h = h + rms_norm(swiglu(u, ffn_w_gate[i], ffn_w_up[i], ffn_w_down[i]), ln_ffn_post[i])
    return h
```

```python
import functools
import math

import jax
import jax.numpy as jnp
import numpy as np
from jax import lax
from jax.experimental import pallas as pl
from jax.experimental.pallas import tpu as pltpu

F32 = jnp.float32
BF16 = jnp.bfloat16

ROPE_THETA = 500000.0
NORM_EPS = 1e-6
NEG_INF = -1e30
POS_BIG = 1e30
M_START = -1e29

N_MIXERS = 4

NSA_HEAD_DIM = 64
NSA_KV_HEADS = 4
CMP_LEN = 32
CMP_STRIDE = 16
SEL_LEN = 64
SEL_TOP = 16
NSA_WIN = 512

MLA_HEADS = 16
MLA_NOPE = 64
MLA_ROPE = 32
MLA_V = 64
MLA_Q_LORA = 768
MLA_KV_LORA = 256

SWA_HEADS = 16
SWA_KV_HEADS = 2
SWA_HEAD_DIM = 64
SWA_WIN = 128

DIFF_HEAD_DIM = 64

V7X_VMEM_LIMIT_BYTES = 52 * 1024 * 1024
LANES = 128
SEL_SEG_MAX = 128


def _cparams(n_axes):
    return pltpu.CompilerParams(dimension_semantics=("arbitrary",) * n_axes,
                                vmem_limit_bytes=V7X_VMEM_LIMIT_BYTES)


def _rms(xf, g):
    return xf * lax.rsqrt(jnp.mean(xf * xf, axis=-1, keepdims=True) + NORM_EPS) * g


def _sigmoid(x):
    return 1.0 / (1.0 + jnp.exp(-x))


def _gelu_tanh(x):
    return 0.5 * x * (1.0 + jnp.tanh(math.sqrt(2.0 / math.pi) * (x + 0.044715 * (x * x * x))))


def _mm_kernel(*refs, pre, act, post, n_chunk, diff_heads, diff_scale):
    it = iter(refs)
    x_ref = next(it)
    g_pre_ref = next(it) if pre else None
    lam_ref = next(it) if diff_heads else None
    w_ref = next(it)
    g_post_ref = next(it) if post else None
    res_ref = next(it) if post else None
    o_ref = next(it)

    if diff_heads:
        d = x_ref.shape[1] // (2 * diff_heads)
        lam = lam_ref[...]
        parts = []
        for h in range(diff_heads):
            o1 = x_ref[:, 2 * h * d:(2 * h + 1) * d].astype(F32)
            o2 = x_ref[:, (2 * h + 1) * d:(2 * h + 2) * d].astype(F32)
            parts.append((_rms(o1 - lam * o2, g_pre_ref[...]) * diff_scale).astype(BF16))
        xn = jnp.concatenate(parts, axis=-1)
    elif pre:
        xn = _rms(x_ref[...].astype(F32), g_pre_ref[...]).astype(BF16)
    else:
        xn = x_ref[...].astype(BF16)

    n = w_ref.shape[1]
    if post:
        y = jnp.dot(xn, w_ref[...], preferred_element_type=F32)
        o_ref[...] = res_ref[...] + _rms(y, g_post_ref[...])
    else:
        for n0 in range(0, n, n_chunk):
            n1 = min(n, n0 + n_chunk)
            y = jnp.dot(xn, w_ref[:, n0:n1], preferred_element_type=F32)
            if act == "gelu":
                y = _gelu_tanh(y)
            o_ref[:, n0:n1] = y.astype(o_ref.dtype)


def _mm(x, w, *, g_pre=None, act=None, g_post=None, resid=None, x_cols=None, out_dtype=BF16,
        lam=None, diff_heads=0, diff_scale=1.0, tm=512, n_chunk=512):
    m = x.shape[0]
    k, n = w.shape
    assert m % tm == 0 and n % LANES == 0
    if x_cols is None:
        x_width, x_blk = x.shape[1], 0
    else:
        start, x_width = x_cols
        assert start % x_width == 0
        x_blk = start // x_width
    pre = g_pre is not None
    post = g_post is not None
    args = [x]
    specs = [pl.BlockSpec((tm, x_width), lambda i: (i, x_blk))]
    if pre:
        args.append(g_pre.reshape(1, -1).astype(F32))
        specs.append(pl.BlockSpec((1, g_pre.shape[-1]), lambda i: (0, 0)))
    if diff_heads:
        args.append(lam.reshape(1, 1).astype(F32))
        specs.append(pl.BlockSpec((1, 1), lambda i: (0, 0)))
    args.append(w)
    specs.append(pl.BlockSpec((k, n), lambda i: (0, 0)))
    if post:
        args += [g_post.reshape(1, -1).astype(F32), resid]
        specs += [pl.BlockSpec((1, n), lambda i: (0, 0)), pl.BlockSpec((tm, n), lambda i: (i, 0))]
        out_dtype = F32
    kern = functools.partial(_mm_kernel, pre=pre, act=act, post=post, n_chunk=n_chunk,
                             diff_heads=diff_heads, diff_scale=diff_scale)
    return pl.pallas_call(
        kern,
        grid=(m // tm,),
        in_specs=specs,
        out_specs=pl.BlockSpec((tm, n), lambda i: (i, 0)),
        out_shape=jax.ShapeDtypeStruct((m, n), out_dtype),
        compiler_params=_cparams(1),
    )(*args)


def _ffn_kernel(h_ref, g_pre_ref, wg_ref, wu_ref, wd_ref, g_post_ref, o_ref, act_ref, *, f_chunk):
    x = h_ref[...]
    u = _rms(x, g_pre_ref[...]).astype(BF16)
    f = wg_ref.shape[1]
    for f0 in range(0, f, f_chunk):
        gate = jnp.dot(u, wg_ref[:, f0:f0 + f_chunk], preferred_element_type=F32)
        up = jnp.dot(u, wu_ref[:, f0:f0 + f_chunk], preferred_element_type=F32)
        act_ref[:, f0:f0 + f_chunk] = (gate * _sigmoid(gate) * up).astype(BF16)
    y = jnp.dot(act_ref[...], wd_ref[...], preferred_element_type=F32)
    o_ref[...] = x + _rms(y, g_post_ref[...])


def _ffn(h, g_pre, wg, wu, wd, g_post, *, tm=512, f_chunk=256):
    m, d = h.shape
    f = wg.shape[1]
    assert m % tm == 0 and f % f_chunk == 0
    row = lambda a: a.reshape(1, -1).astype(F32)
    const = lambda shape: pl.BlockSpec(shape, lambda i: (0, 0))
    return pl.pallas_call(
        functools.partial(_ffn_kernel, f_chunk=f_chunk),
        grid=(m // tm,),
        in_specs=[pl.BlockSpec((tm, d), lambda i: (i, 0)), const((1, d)), const((d, f)), const((d, f)),
                  const((f, d)), const((1, d))],
        out_specs=pl.BlockSpec((tm, d), lambda i: (i, 0)),
        out_shape=jax.ShapeDtypeStruct((m, d), F32),
        scratch_shapes=[pltpu.VMEM((tm, f), BF16)],
        compiler_params=_cparams(1),
    )(h, row(g_pre), wg, wu, wd, row(g_post))


def _flash_kernel(*refs, G, tq, tk, seq, window, nv, has_sink, gate_cols, has_add):
    it = iter(refs)
    sink_ref = next(it) if has_sink else None
    q_ref, k_ref, v_ref = next(it), next(it), next(it)
    gate_ref = next(it) if gate_cols else None
    add_ref = next(it) if has_add else None
    o_ref = next(it)
    m_sc, l_sc, acc_sc = next(it), next(it), next(it)

    hk = pl.program_id(1)
    q0 = pl.program_id(2) * tq
    rows = G * tq
    dk = q_ref.shape[-1]

    if has_sink:
        for g in range(G):
            m_sc[g * tq:(g + 1) * tq, :] = jnp.full((tq, 1), sink_ref[hk * G + g], F32)
        l_sc[...] = jnp.ones_like(l_sc)
    else:
        m_sc[...] = jnp.full_like(m_sc, M_START)
        l_sc[...] = jnp.zeros_like(l_sc)
    acc_sc[...] = jnp.zeros_like(acc_sc)

    def chunk(c, masked):
        k0 = pl.multiple_of(c * tk, tk)
        kc = k_ref[0, 0, pl.ds(k0, tk), :]
        vc = v_ref[0, 0, pl.ds(k0, tk), :]
        if nv > 1:
            q = q_ref[0, 0, k0 // (seq // nv)]
        else:
            q = q_ref[0, 0, 0]
        q = q.reshape(rows, dk)
        s = lax.dot_general(q, kc, (((1,), (1,)), ((), ())), preferred_element_type=F32)
        if masked:
            qpos = q0 + (lax.broadcasted_iota(jnp.int32, (rows, tk), 0) & (tq - 1))
            kpos = k0 + lax.broadcasted_iota(jnp.int32, (rows, tk), 1)
            ok = kpos <= qpos
            if window is not None:
                ok = jnp.logical_and(ok, kpos > qpos - window)
            s = jnp.where(ok, s, NEG_INF)
        m_prev = m_sc[...]
        m_new = jnp.maximum(m_prev, jnp.max(s, axis=-1, keepdims=True))
        alpha = jnp.exp(m_prev - m_new)
        p = jnp.exp(s - m_new)
        l_sc[...] = alpha * l_sc[...] + jnp.sum(p, axis=-1, keepdims=True)
        acc_sc[...] = alpha * acc_sc[...] + jnp.dot(p.astype(BF16), vc, preferred_element_type=F32)
        m_sc[...] = m_new

    n_full = q0 // tk
    if window is None:
        lax.fori_loop(0, n_full, lambda c, _: (chunk(c, False), 0)[1], 0)
    else:
        lo = jnp.maximum(q0 - window + 1, 0) // tk
        lax.fori_loop(lo, n_full, lambda c, _: (chunk(c, True), 0)[1], 0)
    for d in range(max(1, tq // tk)):
        chunk(n_full + d, True)

    out = acc_sc[...] / l_sc[...]
    for g in range(G):
        o = out[g * tq:(g + 1) * tq, :]
        if gate_cols:
            col = gate_cols[g]
            o = o * _sigmoid(gate_ref[0, 0, :, col:col + 1])
        if has_add:
            o = o + add_ref[0, 0, g].astype(F32)
        o_ref[0, 0, g] = o.astype(o_ref.dtype)


def _flash(q, k, v, *, tq, tk, window=None, sinks=None, gate=None, gate_cols=None, add=None,
           k_map=None, v_map=None, out_dtype=BF16):
    b, hk, nv, g, s, dk = q.shape
    dv = v.shape[-1]
    assert s % tq == 0 and s % tk == 0 and (tq % tk == 0 or tk % tq == 0)
    assert tq & (tq - 1) == 0 and (s // nv) % tk == 0
    k_map = k_map or (lambda h: h)
    v_map = v_map or (lambda h: h)
    args, specs = [], []
    if sinks is not None:
        args.append(sinks.astype(F32))
        specs.append(pl.BlockSpec(memory_space=pltpu.SMEM))
    args += [q, k, v]
    specs += [pl.BlockSpec((1, 1, nv, g, tq, dk), lambda bi, h, i: (bi, h, 0, 0, i, 0)),
              pl.BlockSpec((1, 1, s, dk), lambda bi, h, i: (bi, k_map(h), 0, 0)),
              pl.BlockSpec((1, 1, s, dv), lambda bi, h, i: (bi, v_map(h), 0, 0))]
    if gate is not None:
        args.append(gate)
        specs.append(pl.BlockSpec((1, 1, tq, gate.shape[-1]), lambda bi, h, i: (bi, h, i, 0)))
    o_spec = pl.BlockSpec((1, 1, g, tq, dv), lambda bi, h, i: (bi, h, 0, i, 0))
    if add is not None:
        args.append(add)
        specs.append(o_spec)
    kern = functools.partial(_flash_kernel, G=g, tq=tq, tk=tk, seq=s, window=window, nv=nv,
                             has_sink=sinks is not None, gate_cols=gate_cols, has_add=add is not None)
    return pl.pallas_call(
        kern,
        grid=(b, hk, s // tq),
        in_specs=specs,
        out_specs=o_spec,
        out_shape=jax.ShapeDtypeStruct((b, hk, g, s, dv), out_dtype),
        scratch_shapes=[pltpu.VMEM((g * tq, 1), F32), pltpu.VMEM((g * tq, 1), F32),
                        pltpu.VMEM((g * tq, dv), F32)],
        compiler_params=_cparams(3),
    )(*args)


def _nsa_cmp_kernel(q_ref, kc_ref, vc_ref, cov_ref, gate_ref, o_ref, bias_ref, *, G, tq, n_top, gate_cols):
    q0 = pl.program_id(2) * tq
    rows = G * tq
    ncp = kc_ref.shape[2]
    n_sel = cov_ref.shape[0]
    q = q_ref[0, 0, 0].reshape(rows, q_ref.shape[-1])
    s = lax.dot_general(q, kc_ref[0, 0], (((1,), (1,)), ((), ())), preferred_element_type=F32)
    qpos = q0 + (lax.broadcasted_iota(jnp.int32, (rows, ncp), 0) & (tq - 1))
    cmp_end = lax.broadcasted_iota(jnp.int32, (rows, ncp), 1) * CMP_STRIDE + (CMP_LEN - 1)
    valid = cmp_end <= qpos
    s = jnp.where(valid, s, NEG_INF)
    e = jnp.exp(s - jnp.max(s, axis=-1, keepdims=True))
    p = jnp.where(valid, e / jnp.sum(e, axis=-1, keepdims=True), 0.0)
    out = jnp.dot(p.astype(BF16), vc_ref[0, 0], preferred_element_type=F32)
    psum = p[0:tq]
    for g in range(G):
        o = out[g * tq:(g + 1) * tq, :]
        col = gate_cols[g]
        o_ref[0, 0, g] = (o * _sigmoid(gate_ref[0, 0, :, col:col + 1])).astype(o_ref.dtype)
        if g:
            psum = psum + p[g * tq:(g + 1) * tq]

    hi = psum.astype(BF16)
    lo = (psum - hi.astype(F32)).astype(BF16)
    nt = (((1,), (1,)), ((), ()))
    imp = (lax.dot_general(cov_ref[...], hi, nt, preferred_element_type=F32)
           + lax.dot_general(cov_ref[...], lo, nt, preferred_element_type=F32))
    j = lax.broadcasted_iota(jnp.int32, (n_sel, tq), 0)
    qp = q0 + lax.broadcasted_iota(jnp.int32, (n_sel, tq), 1)
    forced = jnp.logical_or(j == lax.shift_right_logical(qp, int(math.log2(SEL_LEN))), j == 0)
    started = j * SEL_LEN <= qp
    imp = jnp.where(forced, POS_BIG, imp)
    imp = jnp.where(started, imp, NEG_INF)
    picked = jnp.zeros((n_sel, tq), F32)
    jf = j.astype(F32)
    for _ in range(n_top):
        best = jnp.max(imp, axis=0, keepdims=True)
        first = jnp.min(jnp.where(imp == best, jf, float(n_sel)), axis=0, keepdims=True)
        hit = jf == first
        picked = jnp.where(hit, 1.0, picked)
        imp = jnp.where(hit, -jnp.inf, imp)
    keep = jnp.logical_and(picked > 0.0, started)
    bias_ref[0, 0] = jnp.where(keep, 0.0, NEG_INF).astype(bias_ref.dtype)


def _nsa_cmp(q, k_cmp, v_cmp, cover_t, gate, gate_cols, *, tq, n_top):
    b, hk, _, g, s, d = q.shape
    ncp = k_cmp.shape[2]
    n_sel = cover_t.shape[0]
    kern = functools.partial(_nsa_cmp_kernel, G=g, tq=tq, n_top=n_top, gate_cols=gate_cols)
    return pl.pallas_call(
        kern,
        grid=(b, hk, s // tq),
        in_specs=[pl.BlockSpec((1, 1, 1, g, tq, d), lambda bi, h, i: (bi, h, 0, 0, i, 0)),
                  pl.BlockSpec((1, 1, ncp, d), lambda bi, h, i: (bi, h, 0, 0)),
                  pl.BlockSpec((1, 1, ncp, d), lambda bi, h, i: (bi, h, 0, 0)),
                  pl.BlockSpec((n_sel, ncp), lambda bi, h, i: (0, 0)),
                  pl.BlockSpec((1, 1, tq, gate.shape[-1]), lambda bi, h, i: (bi, h, i, 0))],
        out_specs=[pl.BlockSpec((1, 1, g, tq, d), lambda bi, h, i: (bi, h, 0, i, 0)),
                   pl.BlockSpec((1, 1, n_sel, tq), lambda bi, h, i: (bi, h, 0, i))],
        out_shape=[jax.ShapeDtypeStruct((b, hk, g, s, d), BF16),
                   jax.ShapeDtypeStruct((b, hk, n_sel, s), BF16)],
        compiler_params=_cparams(3),
    )(q, k_cmp, v_cmp, cover_t, gate)


def _rope(x, pos, rot_dim):
    half = rot_dim // 2
    inv_freq = ROPE_THETA ** (-jnp.arange(half, dtype=F32) / half)
    ang = pos.astype(F32)[:, :, None] * inv_freq
    cos = jnp.cos(ang)[:, :, None, :]
    sin = jnp.sin(ang)[:, :, None, :]
    xf = x.astype(F32)
    x1, x2, rest = xf[..., :half], xf[..., half:rot_dim], xf[..., rot_dim:]
    return jnp.concatenate([x1 * cos - x2 * sin, x2 * cos + x1 * sin, rest], axis=-1)


def _pad_cols(w, mult=LANES):
    n = w.shape[-1]
    return jnp.pad(w, ((0, 0), (0, (-n) % mult)))


def _heads_major(t, hk, g):
    b, s, _, d = t.shape
    return jnp.transpose(t.reshape(b, s, hk, g, d), (0, 2, 3, 1, 4))[:, :, None].astype(BF16)


def _kv_major(t):
    return jnp.transpose(t, (0, 2, 1, 3)).astype(BF16)


def _tokens_major(o):
    b, hk, g, s, d = o.shape
    return jnp.transpose(o, (0, 3, 1, 2, 4)).reshape(b * s, hk * g * d)


def _nsa_layer(h, pos, b, s, g_pre, g_post, w_in, pe_k, w1_k, w2_k, pe_v, w1_v, w2_v, w_out):
    d_model = h.shape[1]
    hd, G = NSA_HEAD_DIM, NSA_KV_HEADS
    H = d_model // hd
    hpg = H // G
    kvw = G * hd
    n_cmp = s // CMP_STRIDE - 1
    n_sel = s // SEL_LEN
    n_top = min(SEL_TOP, n_sel)
    rot = hd // 4
    scale = hd ** -0.5

    n_in = w_in.shape[1]
    proj = _mm(h, _pad_cols(w_in).astype(BF16), g_pre=g_pre)[:, :n_in].reshape(b, s, n_in)
    cuts = [H * hd + i * kvw for i in range(7)]
    q, kc, vc, ks, vs, kw, vw, gl = jnp.split(proj, cuts, axis=-1)
    q = _heads_major(_rope(q.reshape(b, s, H, hd), pos, rot) * scale, G, hpg)
    kc, vc, ks, vs, kw, vw = (t.reshape(b, s, G, hd) for t in (kc, vc, ks, vs, kw, vw))
    ks = _kv_major(_rope(ks, pos, rot))
    kw = _kv_major(_rope(kw, pos, rot))
    vs, vw = _kv_major(vs), _kv_major(vw)
    gates = jnp.transpose(gl.astype(F32).reshape(b, s, G, hpg * 3), (0, 2, 1, 3))
    cols = lambda br: tuple(gi * 3 + br for gi in range(hpg))

    def compress(t, pe, w1, w2):
        ch = t.reshape(b, s // CMP_STRIDE, CMP_STRIDE, G, hd).astype(F32)
        blk = jnp.concatenate([ch[:, :-1], ch[:, 1:]], axis=2) + pe[:, None, :]
        blk = jnp.moveaxis(blk, 3, 2).reshape(b * n_cmp * G, CMP_LEN * hd).astype(BF16)
        rows = blk.shape[0]
        tm = 512
        blk = jnp.pad(blk, ((0, (-rows) % tm), (0, 0)))
        hid = _mm(blk, w1.astype(BF16), act="gelu", tm=tm)
        out = _mm(hid, _pad_cols(w2).astype(BF16), tm=tm)[:rows, :hd]
        return out.reshape(b, n_cmp, G, hd)

    pos_cmp = pos.reshape(b, s // CMP_STRIDE, CMP_STRIDE)[:, 1:, -1]
    k_cmp = _rope(compress(kc, pe_k, w1_k, w2_k), pos_cmp, rot)
    v_cmp = compress(vc, pe_v, w1_v, w2_v)
    ncp = s // CMP_STRIDE
    padc = ((0, 0), (0, 0), (0, ncp - n_cmp), (0, 0))
    k_cmp = jnp.pad(_kv_major(k_cmp), padc)
    v_cmp = jnp.pad(_kv_major(v_cmp), padc)

    ci = np.arange(ncp)[None, :]
    sj = np.arange(n_sel)[:, None] * SEL_LEN
    cover_t = ((ci * CMP_STRIDE <= sj + SEL_LEN - 1) & (ci * CMP_STRIDE + CMP_LEN - 1 >= sj) & (ci < n_cmp))
    cover_t = jnp.asarray(cover_t, BF16)

    tq = 128
    o_cmp, bias_t = _nsa_cmp(q, k_cmp, v_cmp, cover_t, gates, cols(0), tq=tq, n_top=n_top)

    o_win = _flash(q, kw, vw, tq=tq, tk=128, window=NSA_WIN, gate=gates, gate_cols=cols(2), add=o_cmp)

    seg = min(n_sel, SEL_SEG_MAX)
    nv = n_sel // seg
    bias = jnp.transpose(bias_t.reshape(b, G, nv, seg, s), (0, 1, 2, 4, 3))
    bias = jnp.broadcast_to(bias[:, :, :, None], (b, G, nv, hpg, s, seg))
    q_aug = jnp.concatenate([bias, jnp.broadcast_to(q, (b, G, nv, hpg, s, hd))], axis=-1)
    onehot = jnp.asarray((np.arange(s)[:, None] // SEL_LEN) % seg == np.arange(seg)[None, :], BF16)
    k_aug = jnp.concatenate([jnp.broadcast_to(onehot, (b, G, s, seg)), ks], axis=-1)
    o_all = _flash(q_aug, k_aug, vs, tq=tq, tk=min(512, s // nv), gate=gates, gate_cols=cols(1), add=o_win)

    return _mm(_tokens_major(o_all), w_out.astype(BF16), g_post=g_post, resid=h)


def _mla_layer(h, pos, b, s, g_pre, g_post, w_in, q_norm_g, kv_norm_g, w_q_up, w_kv_up, w_out):
    H = MLA_HEADS
    dk = MLA_NOPE + MLA_ROPE
    scale = dk ** -0.5
    lat = _mm(h, _pad_cols(w_in).astype(BF16), g_pre=g_pre)
    q = _mm(lat, w_q_up.astype(BF16), g_pre=q_norm_g, x_cols=(0, MLA_Q_LORA)).reshape(b, s, H, dk)
    kv = _mm(lat, w_kv_up.astype(BF16), g_pre=kv_norm_g, x_cols=(MLA_Q_LORA, MLA_KV_LORA))
    kv = kv.reshape(b, s, H, MLA_NOPE + MLA_V)
    k_pe = lat[:, MLA_Q_LORA + MLA_KV_LORA:MLA_Q_LORA + MLA_KV_LORA + MLA_ROPE].reshape(b, s, 1, MLA_ROPE)
    q = jnp.concatenate([q[..., :MLA_NOPE].astype(F32), _rope(q[..., MLA_NOPE:], pos, MLA_ROPE)], axis=-1)
    k_pe = _rope(k_pe, pos, MLA_ROPE)
    k = jnp.concatenate([kv[..., :MLA_NOPE].astype(F32), jnp.broadcast_to(k_pe, (b, s, H, MLA_ROPE))], axis=-1)
    o = _flash(_heads_major(q * scale, H, 1), _kv_major(k), _kv_major(kv[..., MLA_NOPE:]),
               tq=min(512, s), tk=min(512, s))
    return _mm(_tokens_major(o), w_out.astype(BF16), g_post=g_post, resid=h)


def _swa_layer(h, pos, b, s, g_pre, g_post, w_in, sinks, w_out):
    H, G, hd = SWA_HEADS, SWA_KV_HEADS, SWA_HEAD_DIM
    rot = hd // 4
    proj = _mm(h, w_in.astype(BF16), g_pre=g_pre).reshape(b, s, -1)
    q, k, v = jnp.split(proj, [H * hd, H * hd + G * hd], axis=-1)
    q = _heads_major(_rope(q.reshape(b, s, H, hd), pos, rot) * hd ** -0.5, G, H // G)
    k = _kv_major(_rope(k.reshape(b, s, G, hd), pos, rot))
    v = _kv_major(v.reshape(b, s, G, hd))
    o = _flash(q, k, v, tq=128, tk=128, window=SWA_WIN, sinks=sinks)
    return _mm(_tokens_major(o), w_out.astype(BF16), g_post=g_post, resid=h)


def _diff_layer(h, pos, b, s, g_pre, g_post, w_in, lam_q1, lam_k1, lam_q2, lam_k2, subln_g, w_out, lam_init):
    d = DIFF_HEAD_DIM
    H = h.shape[1] // (2 * d)
    rot = d // 4
    proj = _mm(h, w_in.astype(BF16), g_pre=g_pre).reshape(b, s, -1)
    q, k, v = jnp.split(proj, [H * 2 * d, 2 * H * 2 * d], axis=-1)
    q = _heads_major(_rope(q.reshape(b, s, 2 * H, d), pos, rot) * d ** -0.5, 2 * H, 1)
    k = _kv_major(_rope(k.reshape(b, s, 2 * H, d), pos, rot))
    v = _kv_major(v.reshape(b, s, H, 2 * d))
    lam = (jnp.exp(jnp.sum(lam_q1.astype(F32) * lam_k1.astype(F32)))
           - jnp.exp(jnp.sum(lam_q2.astype(F32) * lam_k2.astype(F32))) + lam_init)
    o = _flash(q, k, v, tq=min(512, s), tk=min(512, s), v_map=lambda hh: hh // 2)
    return _mm(_tokens_major(o), w_out.astype(BF16), g_pre=subln_g, lam=lam, diff_heads=H,
               diff_scale=1.0 - lam_init, g_post=g_post, resid=h)


def kernel(x, positions, ln_mix_pre, ln_mix_post, ln_ffn_pre, ln_ffn_post, ffn_w_gate, ffn_w_up, ffn_w_down,
           nsa_w_in, nsa_cmp_pe_k, nsa_cmp_w1_k, nsa_cmp_w2_k, nsa_cmp_pe_v, nsa_cmp_w1_v, nsa_cmp_w2_v, nsa_w_out,
           mla_w_in, mla_q_norm, mla_kv_norm, mla_w_q_up, mla_w_kv_up, mla_w_out,
           swa_w_in, swa_sinks, swa_w_out,
           diff_w_in, diff_lam_q1, diff_lam_k1, diff_lam_q2, diff_lam_k2, diff_subln, diff_w_out):
    b, s, d_model = x.shape
    depth = ln_mix_pre.shape[0]
    h = x.reshape(b * s, d_model).astype(F32)
    for i in range(depth):
        kind, j = i % N_MIXERS, i // N_MIXERS
        norms = (ln_mix_pre[i], ln_mix_post[i])
        if kind == 0:
            h = _nsa_layer(h, positions, b, s, *norms, nsa_w_in[j], nsa_cmp_pe_k[j], nsa_cmp_w1_k[j],
                           nsa_cmp_w2_k[j], nsa_cmp_pe_v[j], nsa_cmp_w1_v[j], nsa_cmp_w2_v[j], nsa_w_out[j])
        elif kind == 1:
            h = _mla_layer(h, positions, b, s, *norms, mla_w_in[j], mla_q_norm[j], mla_kv_norm[j],
                           mla_w_q_up[j], mla_w_kv_up[j], mla_w_out[j])
        elif kind == 2:
            h = _swa_layer(h, positions, b, s, *norms, swa_w_in[j], swa_sinks[j], swa_w_out[j])
        else:
            lam_init = 0.8 - 0.6 * math.exp(-0.3 * i)
            h = _diff_layer(h, positions, b, s, *norms, diff_w_in[j], diff_lam_q1[j], diff_lam_k1[j],
                            diff_lam_q2[j], diff_lam_k2[j], diff_subln[j], diff_w_out[j], lam_init)
        h = _ffn(h, ln_ffn_pre[i], ffn_w_gate[i].astype(BF16), ffn_w_up[i].astype(BF16),
                 ffn_w_down[i].astype(BF16), ln_ffn_post[i])
    return h.reshape(b, s, d_model)
```

```python
import functools
import math

import jax
import jax.numpy as jnp
import numpy as np
from jax import lax
from jax.experimental import pallas as pl
from jax.experimental.pallas import tpu as pltpu

F32 = jnp.float32
BF16 = jnp.bfloat16

ROPE_THETA = 500000.0
NORM_EPS = 1e-6
NEG_INF = -1e30
POS_BIG = 1e30
M_START = -1e29

N_MIXERS = 4

NSA_HEAD_DIM = 64
NSA_KV_HEADS = 4
CMP_LEN = 32
CMP_STRIDE = 16
SEL_LEN = 64
SEL_TOP = 16
NSA_WIN = 512

MLA_HEADS = 16
MLA_NOPE = 64
MLA_ROPE = 32
MLA_V = 64
MLA_Q_LORA = 768
MLA_KV_LORA = 256

SWA_HEADS = 16
SWA_KV_HEADS = 2
SWA_HEAD_DIM = 64
SWA_WIN = 128

DIFF_HEAD_DIM = 64

V7X_VMEM_LIMIT_BYTES = 52 * 1024 * 1024
LANES = 128
SEL_SEG_MAX = 128


def _cparams(n_axes):
    return pltpu.CompilerParams(dimension_semantics=("arbitrary",) * n_axes,
                                vmem_limit_bytes=V7X_VMEM_LIMIT_BYTES)


def _rms(xf, g):
    return xf * lax.rsqrt(jnp.mean(xf * xf, axis=-1, keepdims=True) + NORM_EPS) * g


def _sigmoid(x):
    return 1.0 / (1.0 + jnp.exp(-x))


def _gelu_tanh(x):
    return 0.5 * x * (1.0 + jnp.tanh(math.sqrt(2.0 / math.pi) * (x + 0.044715 * (x * x * x))))


def _mm_kernel(*refs, pre, act, post, n_chunk, diff_heads, diff_scale):
    it = iter(refs)
    x_ref = next(it)
    g_pre_ref = next(it) if pre else None
    lam_ref = next(it) if diff_heads else None
    w_ref = next(it)
    g_post_ref = next(it) if post else None
    res_ref = next(it) if post else None
    o_ref = next(it)

    if diff_heads:
        d = x_ref.shape[1] // (2 * diff_heads)
        lam = lam_ref[...]
        parts = []
        for h in range(diff_heads):
            o1 = x_ref[:, 2 * h * d:(2 * h + 1) * d].astype(F32)
            o2 = x_ref[:, (2 * h + 1) * d:(2 * h + 2) * d].astype(F32)
            parts.append((_rms(o1 - lam * o2, g_pre_ref[...]) * diff_scale).astype(BF16))
        xn = jnp.concatenate(parts, axis=-1)
    elif pre:
        xn = _rms(x_ref[...].astype(F32), g_pre_ref[...]).astype(BF16)
    else:
        xn = x_ref[...].astype(BF16)

    n = w_ref.shape[1]
    if post:
        y = jnp.dot(xn, w_ref[...], preferred_element_type=F32)
        o_ref[...] = res_ref[...] + _rms(y, g_post_ref[...])
    else:
        for n0 in range(0, n, n_chunk):
            n1 = min(n, n0 + n_chunk)
            y = jnp.dot(xn, w_ref[:, n0:n1], preferred_element_type=F32)
            if act == "gelu":
                y = _gelu_tanh(y)
            o_ref[:, n0:n1] = y.astype(o_ref.dtype)


def _mm(x, w, *, g_pre=None, act=None, g_post=None, resid=None, x_cols=None, out_dtype=BF16,
        lam=None, diff_heads=0, diff_scale=1.0, tm=512, n_chunk=512):
    m = x.shape[0]
    k, n = w.shape
    assert m % tm == 0 and n % LANES == 0
    if x_cols is None:
        x_width, x_blk = x.shape[1], 0
    else:
        start, x_width = x_cols
        assert start % x_width == 0
        x_blk = start // x_width
    pre = g_pre is not None
    post = g_post is not None
    args = [x]
    specs = [pl.BlockSpec((tm, x_width), lambda i: (i, x_blk))]
    if pre:
        args.append(g_pre.reshape(1, -1).astype(F32))
        specs.append(pl.BlockSpec((1, g_pre.shape[-1]), lambda i: (0, 0)))
    if diff_heads:
        args.append(lam.reshape(1, 1).astype(F32))
        specs.append(pl.BlockSpec((1, 1), lambda i: (0, 0)))
    args.append(w)
    specs.append(pl.BlockSpec((k, n), lambda i: (0, 0)))
    if post:
        args += [g_post.reshape(1, -1).astype(F32), resid]
        specs += [pl.BlockSpec((1, n), lambda i: (0, 0)), pl.BlockSpec((tm, n), lambda i: (i, 0))]
        out_dtype = F32
    kern = functools.partial(_mm_kernel, pre=pre, act=act, post=post, n_chunk=n_chunk,
                             diff_heads=diff_heads, diff_scale=diff_scale)
    return pl.pallas_call(
        kern,
        grid=(m // tm,),
        in_specs=specs,
        out_specs=pl.BlockSpec((tm, n), lambda i: (i, 0)),
        out_shape=jax.ShapeDtypeStruct((m, n), out_dtype),
        compiler_params=_cparams(1),
    )(*args)


def _ffn_kernel(h_ref, g_pre_ref, wg_ref, wu_ref, wd_ref, g_post_ref, o_ref, act_ref, *, f_chunk):
    x = h_ref[...]
    u = _rms(x, g_pre_ref[...]).astype(BF16)
    f = wg_ref.shape[1]
    for f0 in range(0, f, f_chunk):
        gate = jnp.dot(u, wg_ref[:, f0:f0 + f_chunk], preferred_element_type=F32)
        up = jnp.dot(u, wu_ref[:, f0:f0 + f_chunk], preferred_element_type=F32)
        act_ref[:, f0:f0 + f_chunk] = (gate * _sigmoid(gate) * up).astype(BF16)
    y = jnp.dot(act_ref[...], wd_ref[...], preferred_element_type=F32)
    o_ref[...] = x + _rms(y, g_post_ref[...])


def _ffn(h, g_pre, wg, wu, wd, g_post, *, tm=512, f_chunk=256):
    m, d = h.shape
    f = wg.shape[1]
    assert m % tm == 0 and f % f_chunk == 0
    row = lambda a: a.reshape(1, -1).astype(F32)
    const = lambda shape: pl.BlockSpec(shape, lambda i: (0, 0))
    return pl.pallas_call(
        functools.partial(_ffn_kernel, f_chunk=f_chunk),
        grid=(m // tm,),
        in_specs=[pl.BlockSpec((tm, d), lambda i: (i, 0)), const((1, d)), const((d, f)), const((d, f)),
                  const((f, d)), const((1, d))],
        out_specs=pl.BlockSpec((tm, d), lambda i: (i, 0)),
        out_shape=jax.ShapeDtypeStruct((m, d), F32),
        scratch_shapes=[pltpu.VMEM((tm, f), BF16)],
        compiler_params=_cparams(1),
    )(h, row(g_pre), wg, wu, wd, row(g_post))


def _flash_kernel(*refs, G, tq, tk, sub, seq, window, nv, has_sink, gate_cols, has_add):
    it = iter(refs)
    sink_ref = next(it) if has_sink else None
    q_ref, k_ref, v_ref = next(it), next(it), next(it)
    gate_ref = next(it) if gate_cols else None
    add_ref = next(it) if has_add else None
    o_ref = next(it)
    m_sc, l_sc, acc_sc = next(it), next(it), next(it)

    hk = pl.program_id(1)
    q0 = pl.program_id(2) * tq
    rows = G * tq
    dk = q_ref.shape[-1]

    if has_sink:
        for g in range(G):
            m_sc[g * tq:(g + 1) * tq, :] = jnp.full((tq, 1), sink_ref[hk * G + g], F32)
        l_sc[...] = jnp.ones_like(l_sc)
    else:
        m_sc[...] = jnp.full_like(m_sc, M_START)
        l_sc[...] = jnp.zeros_like(l_sc)
    acc_sc[...] = jnp.zeros_like(acc_sc)

    def chunk(c, masked):
        k0 = pl.multiple_of(c * tk, tk)
        kc = k_ref[0, 0, pl.ds(k0, tk), :]
        vc = v_ref[0, 0, pl.ds(k0, tk), :]
        var = k0 // (seq // nv) if nv > 1 else 0
        for r0 in range(0, rows, sub):
            if sub >= tq:
                q = q_ref[0, 0, var, r0 // tq:(r0 + sub) // tq].reshape(sub, dk)
            else:
                q = q_ref[0, 0, var, r0 // tq, r0 % tq:r0 % tq + sub, :]
            s = lax.dot_general(q, kc, (((1,), (1,)), ((), ())), preferred_element_type=F32)
            if masked:
                qpos = q0 + ((r0 + lax.broadcasted_iota(jnp.int32, (sub, tk), 0)) & (tq - 1))
                kpos = k0 + lax.broadcasted_iota(jnp.int32, (sub, tk), 1)
                ok = kpos <= qpos
                if window is not None:
                    ok = jnp.logical_and(ok, kpos > qpos - window)
                s = jnp.where(ok, s, NEG_INF)
            blk = slice(r0, r0 + sub)
            m_prev = m_sc[blk, :]
            m_new = jnp.maximum(m_prev, jnp.max(s, axis=-1, keepdims=True))
            alpha = jnp.exp(m_prev - m_new)
            p = jnp.exp(s - m_new)
            l_sc[blk, :] = alpha * l_sc[blk, :] + jnp.sum(p, axis=-1, keepdims=True)
            acc_sc[blk, :] = alpha * acc_sc[blk, :] + jnp.dot(p.astype(BF16), vc, preferred_element_type=F32)
            m_sc[blk, :] = m_new

    n_full = q0 // tk
    if window is None:
        lax.fori_loop(0, n_full, lambda c, _: (chunk(c, False), 0)[1], 0)
    else:
        lo = jnp.maximum(q0 - window + 1, 0) // tk
        lax.fori_loop(lo, n_full, lambda c, _: (chunk(c, True), 0)[1], 0)
    for d in range(max(1, tq // tk)):
        chunk(n_full + d, True)

    out = acc_sc[...] / l_sc[...]
    for g in range(G):
        o = out[g * tq:(g + 1) * tq, :]
        if gate_cols:
            col = gate_cols[g]
            o = o * _sigmoid(gate_ref[0, 0, :, col:col + 1])
        if has_add:
            o = o + add_ref[0, 0, g].astype(F32)
        o_ref[0, 0, g] = o.astype(o_ref.dtype)


def _flash(q, k, v, *, tq, tk, sub=256, window=None, sinks=None, gate=None, gate_cols=None, add=None,
           k_map=None, v_map=None, out_dtype=BF16):
    b, hk, nv, g, s, dk = q.shape
    dv = v.shape[-1]
    assert s % tq == 0 and s % tk == 0 and (tq % tk == 0 or tk % tq == 0)
    assert tq & (tq - 1) == 0 and (s // nv) % tk == 0
    sub = min(sub, g * tq)
    assert (g * tq) % sub == 0 and (sub % tq == 0 or tq % sub == 0)
    k_map = k_map or (lambda h: h)
    v_map = v_map or (lambda h: h)
    args, specs = [], []
    if sinks is not None:
        args.append(sinks.astype(F32))
        specs.append(pl.BlockSpec(memory_space=pltpu.SMEM))
    args += [q, k, v]
    specs += [pl.BlockSpec((1, 1, nv, g, tq, dk), lambda bi, h, i: (bi, h, 0, 0, i, 0)),
              pl.BlockSpec((1, 1, s, dk), lambda bi, h, i: (bi, k_map(h), 0, 0)),
              pl.BlockSpec((1, 1, s, dv), lambda bi, h, i: (bi, v_map(h), 0, 0))]
    if gate is not None:
        args.append(gate)
        specs.append(pl.BlockSpec((1, 1, tq, gate.shape[-1]), lambda bi, h, i: (bi, h, i, 0)))
    o_spec = pl.BlockSpec((1, 1, g, tq, dv), lambda bi, h, i: (bi, h, 0, i, 0))
    if add is not None:
        args.append(add)
        specs.append(o_spec)
    kern = functools.partial(_flash_kernel, G=g, tq=tq, tk=tk, sub=sub, seq=s, window=window, nv=nv,
                             has_sink=sinks is not None, gate_cols=gate_cols, has_add=add is not None)
    return pl.pallas_call(
        kern,
        grid=(b, hk, s // tq),
        in_specs=specs,
        out_specs=o_spec,
        out_shape=jax.ShapeDtypeStruct((b, hk, g, s, dv), out_dtype),
        scratch_shapes=[pltpu.VMEM((g * tq, 1), F32), pltpu.VMEM((g * tq, 1), F32),
                        pltpu.VMEM((g * tq, dv), F32)],
        compiler_params=_cparams(3),
    )(*args)


def _nsa_cmp_kernel(q_ref, kc_ref, vc_ref, cov_ref, gate_ref, o_ref, bias_ref, *, G, tq, n_top, gate_cols):
    q0 = pl.program_id(2) * tq
    rows = G * tq
    ncp = kc_ref.shape[2]
    n_sel = cov_ref.shape[0]
    q = q_ref[0, 0, 0].reshape(rows, q_ref.shape[-1])
    s = lax.dot_general(q, kc_ref[0, 0], (((1,), (1,)), ((), ())), preferred_element_type=F32)
    qpos = q0 + (lax.broadcasted_iota(jnp.int32, (rows, ncp), 0) & (tq - 1))
    cmp_end = lax.broadcasted_iota(jnp.int32, (rows, ncp), 1) * CMP_STRIDE + (CMP_LEN - 1)
    valid = cmp_end <= qpos
    s = jnp.where(valid, s, NEG_INF)
    e = jnp.exp(s - jnp.max(s, axis=-1, keepdims=True))
    p = jnp.where(valid, e / jnp.sum(e, axis=-1, keepdims=True), 0.0)
    out = jnp.dot(p.astype(BF16), vc_ref[0, 0], preferred_element_type=F32)
    psum = p[0:tq]
    for g in range(G):
        o = out[g * tq:(g + 1) * tq, :]
        col = gate_cols[g]
        o_ref[0, 0, g] = (o * _sigmoid(gate_ref[0, 0, :, col:col + 1])).astype(o_ref.dtype)
        if g:
            psum = psum + p[g * tq:(g + 1) * tq]

    hi = psum.astype(BF16)
    lo = (psum - hi.astype(F32)).astype(BF16)
    nt = (((1,), (1,)), ((), ()))
    imp = (lax.dot_general(cov_ref[...], hi, nt, preferred_element_type=F32)
           + lax.dot_general(cov_ref[...], lo, nt, preferred_element_type=F32))
    j = lax.broadcasted_iota(jnp.int32, (n_sel, tq), 0)
    qp = q0 + lax.broadcasted_iota(jnp.int32, (n_sel, tq), 1)
    forced = jnp.logical_or(j == lax.shift_right_logical(qp, int(math.log2(SEL_LEN))), j == 0)
    started = j * SEL_LEN <= qp
    imp = jnp.where(forced, POS_BIG, imp)
    imp = jnp.where(started, imp, NEG_INF)
    picked = jnp.zeros((n_sel, tq), F32)
    jf = j.astype(F32)
    for _ in range(n_top):
        best = jnp.max(imp, axis=0, keepdims=True)
        first = jnp.min(jnp.where(imp == best, jf, float(n_sel)), axis=0, keepdims=True)
        hit = jf == first
        picked = jnp.where(hit, 1.0, picked)
        imp = jnp.where(hit, -jnp.inf, imp)
    keep = jnp.logical_and(picked > 0.0, started)
    bias_ref[0, 0] = jnp.where(keep, 0.0, NEG_INF).astype(bias_ref.dtype)


def _nsa_cmp(q, k_cmp, v_cmp, cover_t, gate, gate_cols, *, tq, n_top):
    b, hk, _, g, s, d = q.shape
    ncp = k_cmp.shape[2]
    n_sel = cover_t.shape[0]
    kern = functools.partial(_nsa_cmp_kernel, G=g, tq=tq, n_top=n_top, gate_cols=gate_cols)
    return pl.pallas_call(
        kern,
        grid=(b, hk, s // tq),
        in_specs=[pl.BlockSpec((1, 1, 1, g, tq, d), lambda bi, h, i: (bi, h, 0, 0, i, 0)),
                  pl.BlockSpec((1, 1, ncp, d), lambda bi, h, i: (bi, h, 0, 0)),
                  pl.BlockSpec((1, 1, ncp, d), lambda bi, h, i: (bi, h, 0, 0)),
                  pl.BlockSpec((n_sel, ncp), lambda bi, h, i: (0, 0)),
                  pl.BlockSpec((1, 1, tq, gate.shape[-1]), lambda bi, h, i: (bi, h, i, 0))],
        out_specs=[pl.BlockSpec((1, 1, g, tq, d), lambda bi, h, i: (bi, h, 0, i, 0)),
                   pl.BlockSpec((1, 1, n_sel, tq), lambda bi, h, i: (bi, h, 0, i))],
        out_shape=[jax.ShapeDtypeStruct((b, hk, g, s, d), BF16),
                   jax.ShapeDtypeStruct((b, hk, n_sel, s), BF16)],
        compiler_params=_cparams(3),
    )(q, k_cmp, v_cmp, cover_t, gate)


def _rope(x, pos, rot_dim):
    half = rot_dim // 2
    inv_freq = ROPE_THETA ** (-jnp.arange(half, dtype=F32) / half)
    ang = pos.astype(F32)[:, :, None] * inv_freq
    cos = jnp.cos(ang)[:, :, None, :]
    sin = jnp.sin(ang)[:, :, None, :]
    xf = x.astype(F32)
    x1, x2, rest = xf[..., :half], xf[..., half:rot_dim], xf[..., rot_dim:]
    return jnp.concatenate([x1 * cos - x2 * sin, x2 * cos + x1 * sin, rest], axis=-1)


def _pad_cols(w, mult=LANES):
    n = w.shape[-1]
    return jnp.pad(w, ((0, 0), (0, (-n) % mult)))


def _heads_major(t, hk, g):
    b, s, _, d = t.shape
    return jnp.transpose(t.reshape(b, s, hk, g, d), (0, 2, 3, 1, 4))[:, :, None].astype(BF16)


def _kv_major(t):
    return jnp.transpose(t, (0, 2, 1, 3)).astype(BF16)


def _tokens_major(o):
    b, hk, g, s, d = o.shape
    return jnp.transpose(o, (0, 3, 1, 2, 4)).reshape(b * s, hk * g * d)


def _nsa_layer(h, pos, b, s, g_pre, g_post, w_in, pe_k, w1_k, w2_k, pe_v, w1_v, w2_v, w_out):
    d_model = h.shape[1]
    hd, G = NSA_HEAD_DIM, NSA_KV_HEADS
    H = d_model // hd
    hpg = H // G
    kvw = G * hd
    n_cmp = s // CMP_STRIDE - 1
    n_sel = s // SEL_LEN
    n_top = min(SEL_TOP, n_sel)
    rot = hd // 4
    scale = hd ** -0.5

    n_in = w_in.shape[1]
    proj = _mm(h, _pad_cols(w_in).astype(BF16), g_pre=g_pre)[:, :n_in].reshape(b, s, n_in)
    cuts = [H * hd + i * kvw for i in range(7)]
    q, kc, vc, ks, vs, kw, vw, gl = jnp.split(proj, cuts, axis=-1)
    q = _heads_major(_rope(q.reshape(b, s, H, hd), pos, rot) * scale, G, hpg)
    kc, vc, ks, vs, kw, vw = (t.reshape(b, s, G, hd) for t in (kc, vc, ks, vs, kw, vw))
    ks = _kv_major(_rope(ks, pos, rot))
    kw = _kv_major(_rope(kw, pos, rot))
    vs, vw = _kv_major(vs), _kv_major(vw)
    gates = jnp.transpose(gl.astype(F32).reshape(b, s, G, hpg * 3), (0, 2, 1, 3))
    cols = lambda br: tuple(gi * 3 + br for gi in range(hpg))

    def compress(t, pe, w1, w2):
        ch = t.reshape(b, s // CMP_STRIDE, CMP_STRIDE, G, hd).astype(F32)
        blk = jnp.concatenate([ch[:, :-1], ch[:, 1:]], axis=2) + pe[:, None, :]
        blk = jnp.moveaxis(blk, 3, 2).reshape(b * n_cmp * G, CMP_LEN * hd).astype(BF16)
        rows = blk.shape[0]
        tm = 512
        blk = jnp.pad(blk, ((0, (-rows) % tm), (0, 0)))
        hid = _mm(blk, w1.astype(BF16), act="gelu", tm=tm)
        out = _mm(hid, _pad_cols(w2).astype(BF16), tm=tm)[:rows, :hd]
        return out.reshape(b, n_cmp, G, hd)

    pos_cmp = pos.reshape(b, s // CMP_STRIDE, CMP_STRIDE)[:, 1:, -1]
    k_cmp = _rope(compress(kc, pe_k, w1_k, w2_k), pos_cmp, rot)
    v_cmp = compress(vc, pe_v, w1_v, w2_v)
    ncp = s // CMP_STRIDE
    padc = ((0, 0), (0, 0), (0, ncp - n_cmp), (0, 0))
    k_cmp = jnp.pad(_kv_major(k_cmp), padc)
    v_cmp = jnp.pad(_kv_major(v_cmp), padc)

    ci = np.arange(ncp)[None, :]
    sj = np.arange(n_sel)[:, None] * SEL_LEN
    cover_t = ((ci * CMP_STRIDE <= sj + SEL_LEN - 1) & (ci * CMP_STRIDE + CMP_LEN - 1 >= sj) & (ci < n_cmp))
    cover_t = jnp.asarray(cover_t, BF16)

    tq = 128
    o_cmp, bias_t = _nsa_cmp(q, k_cmp, v_cmp, cover_t, gates, cols(0), tq=tq, n_top=n_top)

    o_win = _flash(q, kw, vw, tq=tq, tk=128, window=NSA_WIN, gate=gates, gate_cols=cols(2), add=o_cmp)

    seg = min(n_sel, SEL_SEG_MAX)
    nv = n_sel // seg
    bias = jnp.transpose(bias_t.reshape(b, G, nv, seg, s), (0, 1, 2, 4, 3))
    bias = jnp.broadcast_to(bias[:, :, :, None], (b, G, nv, hpg, s, seg))
    q_aug = jnp.concatenate([bias, jnp.broadcast_to(q, (b, G, nv, hpg, s, hd))], axis=-1)
    onehot = jnp.asarray((np.arange(s)[:, None] // SEL_LEN) % seg == np.arange(seg)[None, :], BF16)
    k_aug = jnp.concatenate([jnp.broadcast_to(onehot, (b, G, s, seg)), ks], axis=-1)
    o_all = _flash(q_aug, k_aug, vs, tq=2 * tq, tk=min(512, s // nv), gate=gates, gate_cols=cols(1), add=o_win)

    return _mm(_tokens_major(o_all), w_out.astype(BF16), g_post=g_post, resid=h)


def _mla_layer(h, pos, b, s, g_pre, g_post, w_in, q_norm_g, kv_norm_g, w_q_up, w_kv_up, w_out):
    H = MLA_HEADS
    dk = MLA_NOPE + MLA_ROPE
    scale = dk ** -0.5
    lat = _mm(h, _pad_cols(w_in).astype(BF16), g_pre=g_pre)
    q = _mm(lat, w_q_up.astype(BF16), g_pre=q_norm_g, x_cols=(0, MLA_Q_LORA)).reshape(b, s, H, dk)
    kv = _mm(lat, w_kv_up.astype(BF16), g_pre=kv_norm_g, x_cols=(MLA_Q_LORA, MLA_KV_LORA))
    kv = kv.reshape(b, s, H, MLA_NOPE + MLA_V)
    k_pe = lat[:, MLA_Q_LORA + MLA_KV_LORA:MLA_Q_LORA + MLA_KV_LORA + MLA_ROPE].reshape(b, s, 1, MLA_ROPE)
    q = jnp.concatenate([q[..., :MLA_NOPE].astype(F32), _rope(q[..., MLA_NOPE:], pos, MLA_ROPE)], axis=-1)
    k_pe = _rope(k_pe, pos, MLA_ROPE)
    k = jnp.concatenate([kv[..., :MLA_NOPE].astype(F32), jnp.broadcast_to(k_pe, (b, s, H, MLA_ROPE))], axis=-1)
    o = _flash(_heads_major(q * scale, H, 1), _kv_major(k), _kv_major(kv[..., MLA_NOPE:]),
               tq=min(1024, s), tk=min(512, s))
    return _mm(_tokens_major(o), w_out.astype(BF16), g_post=g_post, resid=h)


def _swa_layer(h, pos, b, s, g_pre, g_post, w_in, sinks, w_out):
    H, G, hd = SWA_HEADS, SWA_KV_HEADS, SWA_HEAD_DIM
    rot = hd // 4
    proj = _mm(h, w_in.astype(BF16), g_pre=g_pre).reshape(b, s, -1)
    q, k, v = jnp.split(proj, [H * hd, H * hd + G * hd], axis=-1)
    q = _heads_major(_rope(q.reshape(b, s, H, hd), pos, rot) * hd ** -0.5, G, H // G)
    k = _kv_major(_rope(k.reshape(b, s, G, hd), pos, rot))
    v = _kv_major(v.reshape(b, s, G, hd))
    o = _flash(q, k, v, tq=128, tk=128, window=SWA_WIN, sinks=sinks)
    return _mm(_tokens_major(o), w_out.astype(BF16), g_post=g_post, resid=h)


def _diff_layer(h, pos, b, s, g_pre, g_post, w_in, lam_q1, lam_k1, lam_q2, lam_k2, subln_g, w_out, lam_init):
    d = DIFF_HEAD_DIM
    H = h.shape[1] // (2 * d)
    rot = d // 4
    proj = _mm(h, w_in.astype(BF16), g_pre=g_pre).reshape(b, s, -1)
    q, k, v = jnp.split(proj, [H * 2 * d, 2 * H * 2 * d], axis=-1)
    q = _heads_major(_rope(q.reshape(b, s, 2 * H, d), pos, rot) * d ** -0.5, 2 * H, 1)
    k = _kv_major(_rope(k.reshape(b, s, 2 * H, d), pos, rot))
    v = _kv_major(v.reshape(b, s, H, 2 * d))
    lam = (jnp.exp(jnp.sum(lam_q1.astype(F32) * lam_k1.astype(F32)))
           - jnp.exp(jnp.sum(lam_q2.astype(F32) * lam_k2.astype(F32))) + lam_init)
    o = _flash(q, k, v, tq=min(1024, s), tk=min(512, s), v_map=lambda hh: hh // 2)
    return _mm(_tokens_major(o), w_out.astype(BF16), g_pre=subln_g, lam=lam, diff_heads=H,
               diff_scale=1.0 - lam_init, g_post=g_post, resid=h)


def kernel(x, positions, ln_mix_pre, ln_mix_post, ln_ffn_pre, ln_ffn_post, ffn_w_gate, ffn_w_up, ffn_w_down,
           nsa_w_in, nsa_cmp_pe_k, nsa_cmp_w1_k, nsa_cmp_w2_k, nsa_cmp_pe_v, nsa_cmp_w1_v, nsa_cmp_w2_v, nsa_w_out,
           mla_w_in, mla_q_norm, mla_kv_norm, mla_w_q_up, mla_w_kv_up, mla_w_out,
           swa_w_in, swa_sinks, swa_w_out,
           diff_w_in, diff_lam_q1, diff_lam_k1, diff_lam_q2, diff_lam_k2, diff_subln, diff_w_out):
    b, s, d_model = x.shape
    depth = ln_mix_pre.shape[0]
    h = x.reshape(b * s, d_model).astype(F32)
    for i in range(depth):
        kind, j = i % N_MIXERS, i // N_MIXERS
        norms = (ln_mix_pre[i], ln_mix_post[i])
        if kind == 0:
            h = _nsa_layer(h, positions, b, s, *norms, nsa_w_in[j], nsa_cmp_pe_k[j], nsa_cmp_w1_k[j],
                           nsa_cmp_w2_k[j], nsa_cmp_pe_v[j], nsa_cmp_w1_v[j], nsa_cmp_w2_v[j], nsa_w_out[j])
        elif kind == 1:
            h = _mla_layer(h, positions, b, s, *norms, mla_w_in[j], mla_q_norm[j], mla_kv_norm[j],
                           mla_w_q_up[j], mla_w_kv_up[j], mla_w_out[j])
        elif kind == 2:
            h = _swa_layer(h, positions, b, s, *norms, swa_w_in[j], swa_sinks[j], swa_w_out[j])
        else:
            lam_init = 0.8 - 0.6 * math.exp(-0.3 * i)
            h = _diff_layer(h, positions, b, s, *norms, diff_w_in[j], diff_lam_q1[j], diff_lam_k1[j],
                            diff_lam_q2[j], diff_lam_k2[j], diff_subln[j], diff_w_out[j], lam_init)
        h = _ffn(h, ln_ffn_pre[i], ffn_w_gate[i].astype(BF16), ffn_w_up[i].astype(BF16),
                 ffn_w_down[i].astype(BF16), ln_ffn_post[i])
    return h.reshape(b, s, d_model)
```

```python
import functools
import math

import jax
import jax.numpy as jnp
import numpy as np
from jax import lax
from jax.experimental import pallas as pl
from jax.experimental.pallas import tpu as pltpu

F32 = jnp.float32
BF16 = jnp.bfloat16

ROPE_THETA = 500000.0
NORM_EPS = 1e-6
NEG_INF = -1e30
POS_BIG = 1e30
M_START = -1e29

N_MIXERS = 4

NSA_HEAD_DIM = 64
NSA_KV_HEADS = 4
CMP_LEN = 32
CMP_STRIDE = 16
SEL_LEN = 64
SEL_TOP = 16
NSA_WIN = 512

MLA_HEADS = 16
MLA_NOPE = 64
MLA_ROPE = 32
MLA_V = 64
MLA_Q_LORA = 768
MLA_KV_LORA = 256

SWA_HEADS = 16
SWA_KV_HEADS = 2
SWA_HEAD_DIM = 64
SWA_WIN = 128

DIFF_HEAD_DIM = 64

V7X_VMEM_LIMIT_BYTES = 52 * 1024 * 1024
LANES = 128
SEL_SEG_MAX = 128


def _cparams(n_axes):
    return pltpu.CompilerParams(dimension_semantics=("arbitrary",) * n_axes,
                                vmem_limit_bytes=V7X_VMEM_LIMIT_BYTES)


def _rms(xf, g):
    return xf * lax.rsqrt(jnp.mean(xf * xf, axis=-1, keepdims=True) + NORM_EPS) * g


def _sigmoid(x):
    return 1.0 / (1.0 + jnp.exp(-x))


def _gelu_tanh(x):
    return 0.5 * x * (1.0 + jnp.tanh(math.sqrt(2.0 / math.pi) * (x + 0.044715 * (x * x * x))))


def _mm_kernel(*refs, pre, act, post, n_chunk, diff_heads, diff_scale):
    it = iter(refs)
    x_ref = next(it)
    g_pre_ref = next(it) if pre else None
    lam_ref = next(it) if diff_heads else None
    w_ref = next(it)
    g_post_ref = next(it) if post else None
    res_ref = next(it) if post else None
    o_ref = next(it)

    if diff_heads:
        d = x_ref.shape[1] // (2 * diff_heads)
        lam = lam_ref[...]
        parts = []
        for h in range(diff_heads):
            o1 = x_ref[:, 2 * h * d:(2 * h + 1) * d].astype(F32)
            o2 = x_ref[:, (2 * h + 1) * d:(2 * h + 2) * d].astype(F32)
            parts.append((_rms(o1 - lam * o2, g_pre_ref[...]) * diff_scale).astype(BF16))
        xn = jnp.concatenate(parts, axis=-1)
    elif pre:
        xn = _rms(x_ref[...].astype(F32), g_pre_ref[...]).astype(BF16)
    else:
        xn = x_ref[...].astype(BF16)

    n = w_ref.shape[1]
    if post:
        y = jnp.dot(xn, w_ref[...], preferred_element_type=F32)
        o_ref[...] = res_ref[...] + _rms(y, g_post_ref[...])
    else:
        for n0 in range(0, n, n_chunk):
            n1 = min(n, n0 + n_chunk)
            y = jnp.dot(xn, w_ref[:, n0:n1], preferred_element_type=F32)
            if act == "gelu":
                y = _gelu_tanh(y)
            o_ref[:, n0:n1] = y.astype(o_ref.dtype)


def _mm(x, w, *, g_pre=None, act=None, g_post=None, resid=None, x_cols=None, out_dtype=BF16,
        lam=None, diff_heads=0, diff_scale=1.0, tm=512, n_chunk=512):
    m = x.shape[0]
    k, n = w.shape
    assert m % tm == 0 and n % LANES == 0
    if x_cols is None:
        x_width, x_blk = x.shape[1], 0
    else:
        start, x_width = x_cols
        assert start % x_width == 0
        x_blk = start // x_width
    pre = g_pre is not None
    post = g_post is not None
    args = [x]
    specs = [pl.BlockSpec((tm, x_width), lambda i: (i, x_blk))]
    if pre:
        args.append(g_pre.reshape(1, -1).astype(F32))
        specs.append(pl.BlockSpec((1, g_pre.shape[-1]), lambda i: (0, 0)))
    if diff_heads:
        args.append(lam.reshape(1, 1).astype(F32))
        specs.append(pl.BlockSpec((1, 1), lambda i: (0, 0)))
    args.append(w)
    specs.append(pl.BlockSpec((k, n), lambda i: (0, 0)))
    if post:
        args += [g_post.reshape(1, -1).astype(F32), resid]
        specs += [pl.BlockSpec((1, n), lambda i: (0, 0)), pl.BlockSpec((tm, n), lambda i: (i, 0))]
        out_dtype = F32
    kern = functools.partial(_mm_kernel, pre=pre, act=act, post=post, n_chunk=n_chunk,
                             diff_heads=diff_heads, diff_scale=diff_scale)
    return pl.pallas_call(
        kern,
        grid=(m // tm,),
        in_specs=specs,
        out_specs=pl.BlockSpec((tm, n), lambda i: (i, 0)),
        out_shape=jax.ShapeDtypeStruct((m, n), out_dtype),
        compiler_params=_cparams(1),
    )(*args)


def _ffn_kernel(h_ref, g_pre_ref, wg_ref, wu_ref, wd_ref, g_post_ref, o_ref, act_ref, *, f_chunk):
    x = h_ref[...]
    u = _rms(x, g_pre_ref[...]).astype(BF16)
    f = wg_ref.shape[1]
    for f0 in range(0, f, f_chunk):
        gate = jnp.dot(u, wg_ref[:, f0:f0 + f_chunk], preferred_element_type=F32)
        up = jnp.dot(u, wu_ref[:, f0:f0 + f_chunk], preferred_element_type=F32)
        act_ref[:, f0:f0 + f_chunk] = (gate * _sigmoid(gate) * up).astype(BF16)
    y = jnp.dot(act_ref[...], wd_ref[...], preferred_element_type=F32)
    o_ref[...] = x + _rms(y, g_post_ref[...])


def _ffn(h, g_pre, wg, wu, wd, g_post, *, tm=512, f_chunk=256):
    m, d = h.shape
    f = wg.shape[1]
    assert m % tm == 0 and f % f_chunk == 0
    row = lambda a: a.reshape(1, -1).astype(F32)
    const = lambda shape: pl.BlockSpec(shape, lambda i: (0, 0))
    return pl.pallas_call(
        functools.partial(_ffn_kernel, f_chunk=f_chunk),
        grid=(m // tm,),
        in_specs=[pl.BlockSpec((tm, d), lambda i: (i, 0)), const((1, d)), const((d, f)), const((d, f)),
                  const((f, d)), const((1, d))],
        out_specs=pl.BlockSpec((tm, d), lambda i: (i, 0)),
        out_shape=jax.ShapeDtypeStruct((m, d), F32),
        scratch_shapes=[pltpu.VMEM((tm, f), BF16)],
        compiler_params=_cparams(1),
    )(h, row(g_pre), wg, wu, wd, row(g_post))


def _flash_kernel(*refs, G, tq, tk, sub, seq, window, nv, has_sink, gate_cols, has_add):
    it = iter(refs)
    sink_ref = next(it) if has_sink else None
    q_ref, k_ref, v_ref = next(it), next(it), next(it)
    gate_ref = next(it) if gate_cols else None
    add_ref = next(it) if has_add else None
    o_ref = next(it)
    m_sc, l_sc, acc_sc = next(it), next(it), next(it)

    hk = pl.program_id(1)
    q0 = pl.program_id(2) * tq
    rows = G * tq
    dk = q_ref.shape[-1]

    if has_sink:
        for g in range(G):
            m_sc[g * tq:(g + 1) * tq, :] = jnp.full((tq, 1), sink_ref[hk * G + g], F32)
        l_sc[...] = jnp.ones_like(l_sc)
    else:
        m_sc[...] = jnp.full_like(m_sc, M_START)
        l_sc[...] = jnp.zeros_like(l_sc)
    acc_sc[...] = jnp.zeros_like(acc_sc)

    def chunk(c, masked):
        k0 = pl.multiple_of(c * tk, tk)
        kc = k_ref[0, 0, pl.ds(k0, tk), :]
        vc = v_ref[0, 0, pl.ds(k0, tk), :]
        var = k0 // (seq // nv) if nv > 1 else 0
        for r0 in range(0, rows, sub):
            if sub >= tq:
                q = q_ref[0, 0, var, r0 // tq:(r0 + sub) // tq].reshape(sub, dk)
            else:
                q = q_ref[0, 0, var, r0 // tq, r0 % tq:r0 % tq + sub, :]
            s = lax.dot_general(q, kc, (((1,), (1,)), ((), ())), preferred_element_type=F32)
            if masked:
                qpos = q0 + ((r0 + lax.broadcasted_iota(jnp.int32, (sub, tk), 0)) & (tq - 1))
                kpos = k0 + lax.broadcasted_iota(jnp.int32, (sub, tk), 1)
                ok = kpos <= qpos
                if window is not None:
                    ok = jnp.logical_and(ok, kpos > qpos - window)
                s = jnp.where(ok, s, NEG_INF)
            blk = slice(r0, r0 + sub)
            m_prev = m_sc[blk, :]
            m_new = jnp.maximum(m_prev, jnp.max(s, axis=-1, keepdims=True))
            alpha = jnp.exp(m_prev - m_new)
            p = jnp.exp(s - m_new)
            l_sc[blk, :] = alpha * l_sc[blk, :] + jnp.sum(p, axis=-1, keepdims=True)
            acc_sc[blk, :] = alpha * acc_sc[blk, :] + jnp.dot(p.astype(BF16), vc, preferred_element_type=F32)
            m_sc[blk, :] = m_new

    n_full = q0 // tk
    if window is None:
        lax.fori_loop(0, n_full, lambda c, _: (chunk(c, False), 0)[1], 0)
    else:
        lo = jnp.maximum(q0 - window + 1, 0) // tk
        lax.fori_loop(lo, n_full, lambda c, _: (chunk(c, True), 0)[1], 0)
    for d in range(max(1, tq // tk)):
        chunk(n_full + d, True)

    out = acc_sc[...] / l_sc[...]
    for g in range(G):
        o = out[g * tq:(g + 1) * tq, :]
        if gate_cols:
            col = gate_cols[g]
            o = o * _sigmoid(gate_ref[0, 0, :, col:col + 1])
        if has_add:
            o = o + add_ref[0, 0, g].astype(F32)
        o_ref[0, 0, g] = o.astype(o_ref.dtype)


def _flash(q, k, v, *, tq, tk, sub=256, window=None, sinks=None, gate=None, gate_cols=None, add=None,
           k_map=None, v_map=None, out_dtype=BF16):
    b, hk, nv, g, s, dk = q.shape
    dv = v.shape[-1]
    assert s % tq == 0 and s % tk == 0 and (tq % tk == 0 or tk % tq == 0)
    assert tq & (tq - 1) == 0 and (s // nv) % tk == 0
    sub = min(sub, g * tq)
    assert (g * tq) % sub == 0 and (sub % tq == 0 or tq % sub == 0)
    k_map = k_map or (lambda h: h)
    v_map = v_map or (lambda h: h)
    args, specs = [], []
    if sinks is not None:
        args.append(sinks.astype(F32))
        specs.append(pl.BlockSpec(memory_space=pltpu.SMEM))
    args += [q, k, v]
    specs += [pl.BlockSpec((1, 1, nv, g, tq, dk), lambda bi, h, i: (bi, h, 0, 0, i, 0)),
              pl.BlockSpec((1, 1, s, dk), lambda bi, h, i: (bi, k_map(h), 0, 0)),
              pl.BlockSpec((1, 1, s, dv), lambda bi, h, i: (bi, v_map(h), 0, 0))]
    if gate is not None:
        args.append(gate)
        specs.append(pl.BlockSpec((1, 1, tq, gate.shape[-1]), lambda bi, h, i: (bi, h, i, 0)))
    o_spec = pl.BlockSpec((1, 1, g, tq, dv), lambda bi, h, i: (bi, h, 0, i, 0))
    if add is not None:
        args.append(add)
        specs.append(o_spec)
    kern = functools.partial(_flash_kernel, G=g, tq=tq, tk=tk, sub=sub, seq=s, window=window, nv=nv,
                             has_sink=sinks is not None, gate_cols=gate_cols, has_add=add is not None)
    return pl.pallas_call(
        kern,
        grid=(b, hk, s // tq),
        in_specs=specs,
        out_specs=o_spec,
        out_shape=jax.ShapeDtypeStruct((b, hk, g, s, dv), out_dtype),
        scratch_shapes=[pltpu.VMEM((g * tq, 1), F32), pltpu.VMEM((g * tq, 1), F32),
                        pltpu.VMEM((g * tq, dv), F32)],
        compiler_params=_cparams(3),
    )(*args)


def _nsa_cmp_kernel(q_ref, kc_ref, vc_ref, cov_ref, gate_ref, o_ref, bias_ref, *, G, tq, n_top, gate_cols):
    q0 = pl.program_id(2) * tq
    rows = G * tq
    ncp = kc_ref.shape[2]
    n_sel = cov_ref.shape[0]
    q = q_ref[0, 0, 0].reshape(rows, q_ref.shape[-1])
    s = lax.dot_general(q, kc_ref[0, 0], (((1,), (1,)), ((), ())), preferred_element_type=F32)
    qpos = q0 + (lax.broadcasted_iota(jnp.int32, (rows, ncp), 0) & (tq - 1))
    cmp_end = lax.broadcasted_iota(jnp.int32, (rows, ncp), 1) * CMP_STRIDE + (CMP_LEN - 1)
    valid = cmp_end <= qpos
    s = jnp.where(valid, s, NEG_INF)
    e = jnp.exp(s - jnp.max(s, axis=-1, keepdims=True))
    p = jnp.where(valid, e / jnp.sum(e, axis=-1, keepdims=True), 0.0)
    out = jnp.dot(p.astype(BF16), vc_ref[0, 0], preferred_element_type=F32)
    psum = p[0:tq]
    for g in range(G):
        o = out[g * tq:(g + 1) * tq, :]
        col = gate_cols[g]
        o_ref[0, 0, g] = (o * _sigmoid(gate_ref[0, 0, :, col:col + 1])).astype(o_ref.dtype)
        if g:
            psum = psum + p[g * tq:(g + 1) * tq]

    hi = psum.astype(BF16)
    lo = (psum - hi.astype(F32)).astype(BF16)
    nt = (((1,), (1,)), ((), ()))
    imp = (lax.dot_general(cov_ref[...], hi, nt, preferred_element_type=F32)
           + lax.dot_general(cov_ref[...], lo, nt, preferred_element_type=F32))
    j = lax.broadcasted_iota(jnp.int32, (n_sel, tq), 0)
    qp = q0 + lax.broadcasted_iota(jnp.int32, (n_sel, tq), 1)
    forced = jnp.logical_or(j == lax.shift_right_logical(qp, int(math.log2(SEL_LEN))), j == 0)
    started = j * SEL_LEN <= qp
    imp = jnp.where(forced, POS_BIG, imp)
    imp = jnp.where(started, imp, NEG_INF)
    picked = jnp.zeros((n_sel, tq), F32)
    jf = j.astype(F32)
    for _ in range(n_top):
        best = jnp.max(imp, axis=0, keepdims=True)
        first = jnp.min(jnp.where(imp == best, jf, float(n_sel)), axis=0, keepdims=True)
        hit = jf == first
        picked = jnp.where(hit, 1.0, picked)
        imp = jnp.where(hit, -jnp.inf, imp)
    keep = jnp.logical_and(picked > 0.0, started)
    bias_ref[0, 0] = jnp.where(keep, 0.0, NEG_INF).astype(bias_ref.dtype)


def _nsa_cmp(q, k_cmp, v_cmp, cover_t, gate, gate_cols, *, tq, n_top):
    b, hk, _, g, s, d = q.shape
    ncp = k_cmp.shape[2]
    n_sel = cover_t.shape[0]
    kern = functools.partial(_nsa_cmp_kernel, G=g, tq=tq, n_top=n_top, gate_cols=gate_cols)
    return pl.pallas_call(
        kern,
        grid=(b, hk, s // tq),
        in_specs=[pl.BlockSpec((1, 1, 1, g, tq, d), lambda bi, h, i: (bi, h, 0, 0, i, 0)),
                  pl.BlockSpec((1, 1, ncp, d), lambda bi, h, i: (bi, h, 0, 0)),
                  pl.BlockSpec((1, 1, ncp, d), lambda bi, h, i: (bi, h, 0, 0)),
                  pl.BlockSpec((n_sel, ncp), lambda bi, h, i: (0, 0)),
                  pl.BlockSpec((1, 1, tq, gate.shape[-1]), lambda bi, h, i: (bi, h, i, 0))],
        out_specs=[pl.BlockSpec((1, 1, g, tq, d), lambda bi, h, i: (bi, h, 0, i, 0)),
                   pl.BlockSpec((1, 1, n_sel, tq), lambda bi, h, i: (bi, h, 0, i))],
        out_shape=[jax.ShapeDtypeStruct((b, hk, g, s, d), BF16),
                   jax.ShapeDtypeStruct((b, hk, n_sel, s), BF16)],
        compiler_params=_cparams(3),
    )(q, k_cmp, v_cmp, cover_t, gate)


def _rope(x, pos, rot_dim):
    half = rot_dim // 2
    inv_freq = ROPE_THETA ** (-jnp.arange(half, dtype=F32) / half)
    ang = pos.astype(F32)[:, :, None] * inv_freq
    cos = jnp.cos(ang)[:, :, None, :]
    sin = jnp.sin(ang)[:, :, None, :]
    xf = x.astype(F32)
    x1, x2, rest = xf[..., :half], xf[..., half:rot_dim], xf[..., rot_dim:]
    return jnp.concatenate([x1 * cos - x2 * sin, x2 * cos + x1 * sin, rest], axis=-1)


def _pad_cols(w, mult=LANES):
    n = w.shape[-1]
    return jnp.pad(w, ((0, 0), (0, (-n) % mult)))


def _heads_major(t, hk, g):
    b, s, _, d = t.shape
    return jnp.transpose(t.reshape(b, s, hk, g, d), (0, 2, 3, 1, 4))[:, :, None].astype(BF16)


def _kv_major(t):
    return jnp.transpose(t, (0, 2, 1, 3)).astype(BF16)


def _tokens_major(o):
    b, hk, g, s, d = o.shape
    return jnp.transpose(o, (0, 3, 1, 2, 4)).reshape(b * s, hk * g * d)


def _nsa_layer(h, pos, b, s, g_pre, g_post, w_in, pe_k, w1_k, w2_k, pe_v, w1_v, w2_v, w_out):
    d_model = h.shape[1]
    hd, G = NSA_HEAD_DIM, NSA_KV_HEADS
    H = d_model // hd
    hpg = H // G
    kvw = G * hd
    n_cmp = s // CMP_STRIDE - 1
    n_sel = s // SEL_LEN
    n_top = min(SEL_TOP, n_sel)
    rot = hd // 4
    scale = hd ** -0.5

    n_in = w_in.shape[1]
    proj = _mm(h, _pad_cols(w_in).astype(BF16), g_pre=g_pre)[:, :n_in].reshape(b, s, n_in)
    cuts = [H * hd + i * kvw for i in range(7)]
    q, kc, vc, ks, vs, kw, vw, gl = jnp.split(proj, cuts, axis=-1)
    q = _heads_major(_rope(q.reshape(b, s, H, hd), pos, rot) * scale, G, hpg)
    kc, vc, ks, vs, kw, vw = (t.reshape(b, s, G, hd) for t in (kc, vc, ks, vs, kw, vw))
    ks = _kv_major(_rope(ks, pos, rot))
    kw = _kv_major(_rope(kw, pos, rot))
    vs, vw = _kv_major(vs), _kv_major(vw)
    gates = jnp.transpose(gl.astype(F32).reshape(b, s, G, hpg * 3), (0, 2, 1, 3))
    cols = lambda br: tuple(gi * 3 + br for gi in range(hpg))

    def compress(t, pe, w1, w2):
        ch = t.reshape(b, s // CMP_STRIDE, CMP_STRIDE, G, hd).astype(F32)
        blk = jnp.concatenate([ch[:, :-1], ch[:, 1:]], axis=2) + pe[:, None, :]
        blk = jnp.moveaxis(blk, 3, 2).reshape(b * n_cmp * G, CMP_LEN * hd).astype(BF16)
        rows = blk.shape[0]
        tm = 512
        blk = jnp.pad(blk, ((0, (-rows) % tm), (0, 0)))
        hid = _mm(blk, w1.astype(BF16), act="gelu", tm=tm)
        out = _mm(hid, _pad_cols(w2).astype(BF16), tm=tm)[:rows, :hd]
        return out.reshape(b, n_cmp, G, hd)

    pos_cmp = pos.reshape(b, s // CMP_STRIDE, CMP_STRIDE)[:, 1:, -1]
    k_cmp = _rope(compress(kc, pe_k, w1_k, w2_k), pos_cmp, rot)
    v_cmp = compress(vc, pe_v, w1_v, w2_v)
    ncp = s // CMP_STRIDE
    padc = ((0, 0), (0, 0), (0, ncp - n_cmp), (0, 0))
    k_cmp = jnp.pad(_kv_major(k_cmp), padc)
    v_cmp = jnp.pad(_kv_major(v_cmp), padc)

    ci = np.arange(ncp)[None, :]
    sj = np.arange(n_sel)[:, None] * SEL_LEN
    cover_t = ((ci * CMP_STRIDE <= sj + SEL_LEN - 1) & (ci * CMP_STRIDE + CMP_LEN - 1 >= sj) & (ci < n_cmp))
    cover_t = jnp.asarray(cover_t, BF16)

    tq = 128
    o_cmp, bias_t = _nsa_cmp(q, k_cmp, v_cmp, cover_t, gates, cols(0), tq=tq, n_top=n_top)

    o_win = _flash(q, kw, vw, tq=tq, tk=128, sub=hpg * tq, window=NSA_WIN, gate=gates, gate_cols=cols(2),
                   add=o_cmp)

    seg = min(n_sel, SEL_SEG_MAX)
    nv = n_sel // seg
    bias = jnp.transpose(bias_t.reshape(b, G, nv, seg, s), (0, 1, 2, 4, 3))
    bias = jnp.broadcast_to(bias[:, :, :, None], (b, G, nv, hpg, s, seg))
    q_aug = jnp.concatenate([bias, jnp.broadcast_to(q, (b, G, nv, hpg, s, hd))], axis=-1)
    onehot = jnp.asarray((np.arange(s)[:, None] // SEL_LEN) % seg == np.arange(seg)[None, :], BF16)
    k_aug = jnp.concatenate([jnp.broadcast_to(onehot, (b, G, s, seg)), ks], axis=-1)
    o_all = _flash(q_aug, k_aug, vs, tq=2 * tq, tk=min(512, s // nv), gate=gates, gate_cols=cols(1), add=o_win)

    return _mm(_tokens_major(o_all), w_out.astype(BF16), g_post=g_post, resid=h)


def _mla_layer(h, pos, b, s, g_pre, g_post, w_in, q_norm_g, kv_norm_g, w_q_up, w_kv_up, w_out):
    H = MLA_HEADS
    dk = MLA_NOPE + MLA_ROPE
    scale = dk ** -0.5
    lat = _mm(h, _pad_cols(w_in).astype(BF16), g_pre=g_pre)
    q = _mm(lat, w_q_up.astype(BF16), g_pre=q_norm_g, x_cols=(0, MLA_Q_LORA)).reshape(b, s, H, dk)
    kv = _mm(lat, w_kv_up.astype(BF16), g_pre=kv_norm_g, x_cols=(MLA_Q_LORA, MLA_KV_LORA))
    kv = kv.reshape(b, s, H, MLA_NOPE + MLA_V)
    k_pe = lat[:, MLA_Q_LORA + MLA_KV_LORA:MLA_Q_LORA + MLA_KV_LORA + MLA_ROPE].reshape(b, s, 1, MLA_ROPE)
    q = jnp.concatenate([q[..., :MLA_NOPE].astype(F32), _rope(q[..., MLA_NOPE:], pos, MLA_ROPE)], axis=-1)
    k_pe = _rope(k_pe, pos, MLA_ROPE)
    k = jnp.concatenate([kv[..., :MLA_NOPE].astype(F32), jnp.broadcast_to(k_pe, (b, s, H, MLA_ROPE))], axis=-1)
    o = _flash(_heads_major(q * scale, H, 1), _kv_major(k), _kv_major(kv[..., MLA_NOPE:]),
               tq=min(1024, s), tk=min(512, s))
    return _mm(_tokens_major(o), w_out.astype(BF16), g_post=g_post, resid=h)


def _swa_layer(h, pos, b, s, g_pre, g_post, w_in, sinks, w_out):
    H, G, hd = SWA_HEADS, SWA_KV_HEADS, SWA_HEAD_DIM
    rot = hd // 4
    proj = _mm(h, w_in.astype(BF16), g_pre=g_pre).reshape(b, s, -1)
    q, k, v = jnp.split(proj, [H * hd, H * hd + G * hd], axis=-1)
    q = _heads_major(_rope(q.reshape(b, s, H, hd), pos, rot) * hd ** -0.5, G, H // G)
    k = _kv_major(_rope(k.reshape(b, s, G, hd), pos, rot))
    v = _kv_major(v.reshape(b, s, G, hd))
    o = _flash(q, k, v, tq=128, tk=128, window=SWA_WIN, sinks=sinks)
    return _mm(_tokens_major(o), w_out.astype(BF16), g_post=g_post, resid=h)


def _diff_layer(h, pos, b, s, g_pre, g_post, w_in, lam_q1, lam_k1, lam_q2, lam_k2, subln_g, w_out, lam_init):
    d = DIFF_HEAD_DIM
    H = h.shape[1] // (2 * d)
    rot = d // 4
    proj = _mm(h, w_in.astype(BF16), g_pre=g_pre).reshape(b, s, -1)
    q, k, v = jnp.split(proj, [H * 2 * d, 2 * H * 2 * d], axis=-1)
    q = _heads_major(_rope(q.reshape(b, s, 2 * H, d), pos, rot) * d ** -0.5, 2 * H, 1)
    k = _kv_major(_rope(k.reshape(b, s, 2 * H, d), pos, rot))
    v = _kv_major(v.reshape(b, s, H, 2 * d))
    lam = (jnp.exp(jnp.sum(lam_q1.astype(F32) * lam_k1.astype(F32)))
           - jnp.exp(jnp.sum(lam_q2.astype(F32) * lam_k2.astype(F32))) + lam_init)
    o = _flash(q, k, v, tq=min(1024, s), tk=min(512, s), v_map=lambda hh: hh // 2)
    return _mm(_tokens_major(o), w_out.astype(BF16), g_pre=subln_g, lam=lam, diff_heads=H,
               diff_scale=1.0 - lam_init, g_post=g_post, resid=h)


def kernel(x, positions, ln_mix_pre, ln_mix_post, ln_ffn_pre, ln_ffn_post, ffn_w_gate, ffn_w_up, ffn_w_down,
           nsa_w_in, nsa_cmp_pe_k, nsa_cmp_w1_k, nsa_cmp_w2_k, nsa_cmp_pe_v, nsa_cmp_w1_v, nsa_cmp_w2_v, nsa_w_out,
           mla_w_in, mla_q_norm, mla_kv_norm, mla_w_q_up, mla_w_kv_up, mla_w_out,
           swa_w_in, swa_sinks, swa_w_out,
           diff_w_in, diff_lam_q1, diff_lam_k1, diff_lam_q2, diff_lam_k2, diff_subln, diff_w_out):
    b, s, d_model = x.shape
    depth = ln_mix_pre.shape[0]
    h = x.reshape(b * s, d_model).astype(F32)
    for i in range(depth):
        kind, j = i % N_MIXERS, i // N_MIXERS
        norms = (ln_mix_pre[i], ln_mix_post[i])
        if kind == 0:
            h = _nsa_layer(h, positions, b, s, *norms, nsa_w_in[j], nsa_cmp_pe_k[j], nsa_cmp_w1_k[j],
                           nsa_cmp_w2_k[j], nsa_cmp_pe_v[j], nsa_cmp_w1_v[j], nsa_cmp_w2_v[j], nsa_w_out[j])
        elif kind == 1:
            h = _mla_layer(h, positions, b, s, *norms, mla_w_in[j], mla_q_norm[j], mla_kv_norm[j],
                           mla_w_q_up[j], mla_w_kv_up[j], mla_w_out[j])
        elif kind == 2:
            h = _swa_layer(h, positions, b, s, *norms, swa_w_in[j], swa_sinks[j], swa_w_out[j])
        else:
            lam_init = 0.8 - 0.6 * math.exp(-0.3 * i)
            h = _diff_layer(h, positions, b, s, *norms, diff_w_in[j], diff_lam_q1[j], diff_lam_k1[j],
                            diff_lam_q2[j], diff_lam_k2[j], diff_subln[j], diff_w_out[j], lam_init)
        h = _ffn(h, ln_ffn_pre[i], ffn_w_gate[i].astype(BF16), ffn_w_up[i].astype(BF16),
                 ffn_w_down[i].astype(BF16), ln_ffn_post[i])
    return h.reshape(b, s, d_model)
```
